```python
import jax, jax.numpy as jnp
from jax import lax
import numpy as np

D_MODEL = 1024
BATCH = 8
SEQ = 2048
DEPTH = 2
DEC_BATCH = 32
DEC_SEQ = 1
PAST_LEN = 16384
PAGE_SIZE = 128

N_MIXERS = 2
N_ATT_LAYERS = (DEPTH + N_MIXERS - 1) // N_MIXERS
N_RG_LAYERS = DEPTH // N_MIXERS
SB_HEADS = 16
SB_HEAD_DIM = D_MODEL // SB_HEADS
D_ATT = SB_HEADS * SB_HEAD_DIM
Q_BLOCK = 128
SB_BIAS_LO = -9.0
SB_BIAS_HI = -3.0
D_RNN = D_MODEL
RG_BLOCKS = 4
RG_BLOCK_W = D_RNN // RG_BLOCKS
CONV_W = 4
RG_C = 8.0
EPS = 1e-6

kernel_name = 'sb_attn_rglru_hybrid_step'


def _rmsnorm(x, g):
    xf = x.astype(jnp.float32)
    y = xf * lax.rsqrt(jnp.mean(xf * xf, axis=-1, keepdims=True) + EPS)
    return (y * g.astype(jnp.float32)).astype(x.dtype)


def _modulated_norm(x, c, g, w, b):
    m = (jax.nn.silu(c) @ w + b)[:, None, :]
    shift, scale, gate = jnp.split(m, 3, axis=-1)
    return _rmsnorm(x, g) * (1 + scale) + shift, gate


def _sb_weights(z, mask, carry):
    log_keep = jnp.where(mask, jax.nn.log_sigmoid(-z), 0.0)
    suffix = lax.cumsum(log_keep, axis=z.ndim - 1, reverse=True) - log_keep + carry
    a = jnp.where(mask, jnp.exp(jax.nn.log_sigmoid(z) + suffix), 0.0)
    return a, jnp.sum(log_keep, axis=-1, keepdims=True)


def _att_project(h, w_in):
    b, t, _ = h.shape
    q, k, v, g = jnp.split(h @ w_in, 4, axis=-1)
    shp = (b, t, SB_HEADS, SB_HEAD_DIM)
    return q.reshape(shp), k.reshape(shp), v.reshape(shp), g


def _att_out(o, g, w_out):
    b, t = o.shape[:2]
    return (o.reshape(b, t, D_ATT) * jax.nn.silu(g)) @ w_out


def _sb_logits(q, k, bias):
    z = jnp.einsum('bthd,bshd->bhts', q.astype(jnp.float32), k.astype(jnp.float32)) * (SB_HEAD_DIM ** -0.5)
    return z + bias.astype(jnp.float32)[None, :, None, None]


def _sb_prompt(q, k, v, bias):
    b, s = q.shape[:2]
    nb = s // Q_BLOCK
    pos = jnp.arange(s)
    qb = q.reshape(b, nb, Q_BLOCK, SB_HEADS, SB_HEAD_DIM).transpose(1, 0, 2, 3, 4)
    qpos = pos.reshape(nb, Q_BLOCK)

    def blk(args):
        qi, pi = args
        z = _sb_logits(qi, k, bias)
        mask = pos[None, :] < pi[:, None]
        a, _ = _sb_weights(z, mask, 0.0)
        return jnp.einsum('bhts,bshd->bthd', a.astype(v.dtype), v)

    o = lax.map(blk, (qb, qpos))
    return o.transpose(1, 0, 2, 3, 4).reshape(b, s, SB_HEADS, SB_HEAD_DIM)


def _sb_sample(q, k_new, v_new, k_past, v_past, bias):
    t = q.shape[1]
    z_new = _sb_logits(q, k_new, bias)
    idx = jnp.arange(t)
    mask_new = idx[None, :] < idx[:, None]
    a_new, tot_new = _sb_weights(z_new, mask_new, 0.0)
    z_past = _sb_logits(q, k_past, bias)
    mask_past = jnp.ones(z_past.shape[-2:], dtype=bool)
    a_past, _ = _sb_weights(z_past, mask_past, tot_new)
    o = jnp.einsum('bhts,bshd->bthd', a_past.astype(v_past.dtype), v_past)
    o = o + jnp.einsum('bhts,bshd->bthd', a_new.astype(v_new.dtype), v_new)
    return o


def _causal_conv(x, buf, w, b):
    t = x.shape[1]
    xx = jnp.concatenate([buf.astype(x.dtype), x], axis=1)
    y = b + sum(w[i] * xx[:, i:i + t] for i in range(CONV_W))
    return y, xx[:, -(CONV_W - 1):]


def _rg_lru(x, h0, w_a, b_a, w_x, b_x, lam):
    bsz, t, _ = x.shape
    xf = x.astype(jnp.float32)
    xb = xf.reshape(bsz, t, RG_BLOCKS, RG_BLOCK_W)
    r = jax.nn.sigmoid(jnp.einsum('btni,nij->btnj', xb, w_a).reshape(bsz, t, D_RNN) + b_a)
    ig = jax.nn.sigmoid(jnp.einsum('btni,nij->btnj', xb, w_x).reshape(bsz, t, D_RNN) + b_x)
    log_a = RG_C * r * jax.nn.log_sigmoid(lam.astype(jnp.float32))
    a = jnp.exp(log_a)
    u = jnp.sqrt(-jnp.expm1(2.0 * log_a)) * (ig * xf)

    def step(h, au):
        a_t, u_t = au
        h = a_t * h + u_t
        return h, h

    h_last, hs = lax.scan(step, h0.astype(jnp.float32), (a.transpose(1, 0, 2), u.transpose(1, 0, 2)))
    return hs.transpose(1, 0, 2).astype(x.dtype), h_last.astype(x.dtype)


def _rg_branch(h, conv_buf, h0, w_in, conv_w, conv_b, w_a, b_a, w_x, b_x, lam, w_out):
    xbr, gbr = jnp.split(h @ w_in, 2, axis=-1)
    xc, conv_tail = _causal_conv(xbr, conv_buf, conv_w, conv_b)
    y, h_last = _rg_lru(xc, h0, w_a, b_a, w_x, b_x, lam)
    return (y * jax.nn.silu(gbr)) @ w_out, conv_tail, h_last


def setup_inputs(seed: int = 0) -> dict:
    key = jax.random.key(seed)
    ks = jax.random.split(key, 26)
    n_pages = PAST_LEN // PAGE_SIZE
    n_used = DEC_BATCH * n_pages
    n_pool = (n_used * 5) // 4
    f = jnp.float32
    nrm = lambda k, shp, s: jax.random.normal(k, shp, f) * s
    a8 = jax.random.uniform(ks[20], (N_RG_LAYERS, D_RNN), f, 0.9, 0.999)
    a1 = a8 ** (1.0 / RG_C)
    lam = jnp.log(a1) - jnp.log1p(-a1)
    sb_bias = jnp.linspace(SB_BIAS_LO, SB_BIAS_HI, SB_HEADS, dtype=f)[None, :] + nrm(ks[24], (N_ATT_LAYERS, SB_HEADS), 0.1)
    page_table = jax.random.permutation(ks[6], n_pool)[:n_used].reshape(DEC_BATCH, n_pages).astype(jnp.int32)
    return {
        'x_prompt': nrm(ks[0], (BATCH, SEQ, D_MODEL), 1.0),
        'x_sample': nrm(ks[1], (DEC_BATCH, DEC_SEQ, D_MODEL), 1.0),
        'c_prompt': nrm(ks[2], (BATCH, D_MODEL), 1.0),
        'c_sample': nrm(ks[3], (DEC_BATCH, D_MODEL), 1.0),
        'cache_k': nrm(ks[4], (N_ATT_LAYERS, n_pool, PAGE_SIZE, SB_HEADS, SB_HEAD_DIM), 1.0),
        'cache_v': nrm(ks[5], (N_ATT_LAYERS, n_pool, PAGE_SIZE, SB_HEADS, SB_HEAD_DIM), 1.0),
        'page_table': page_table,
        'state_conv': nrm(ks[7], (N_RG_LAYERS, DEC_BATCH, CONV_W - 1, D_RNN), 1.0),
        'state_h': nrm(ks[8], (N_RG_LAYERS, DEC_BATCH, D_RNN), 0.5),
        'norm_g': 1.0 + nrm(ks[9], (DEPTH, D_MODEL), 0.02),
        'mod_w': nrm(ks[10], (DEPTH, D_MODEL, 3 * D_MODEL), 0.5 * D_MODEL ** -0.5),
        'mod_b': nrm(ks[11], (DEPTH, 3 * D_MODEL), 0.02),
        'att_w_in': nrm(ks[12], (N_ATT_LAYERS, D_MODEL, 4 * D_ATT), D_MODEL ** -0.5),
        'att_bias': sb_bias,
        'att_w_out': nrm(ks[13], (N_ATT_LAYERS, D_ATT, D_MODEL), D_ATT ** -0.5),
        'rg_w_in': nrm(ks[14], (N_RG_LAYERS, D_MODEL, 2 * D_RNN), D_MODEL ** -0.5),
        'rg_conv_w': nrm(ks[15], (N_RG_LAYERS, CONV_W, D_RNN), CONV_W ** -0.5),
        'rg_conv_b': nrm(ks[16], (N_RG_LAYERS, D_RNN), 0.02),
        'rg_w_a': nrm(ks[17], (N_RG_LAYERS, RG_BLOCKS, RG_BLOCK_W, RG_BLOCK_W), RG_BLOCK_W ** -0.5),
        'rg_b_a': nrm(ks[18], (N_RG_LAYERS, D_RNN), 0.02),
        'rg_w_x': nrm(ks[19], (N_RG_LAYERS, RG_BLOCKS, RG_BLOCK_W, RG_BLOCK_W), RG_BLOCK_W ** -0.5),
        'rg_b_x': nrm(ks[21], (N_RG_LAYERS, D_RNN), 0.02),
        'rg_lambda': lam,
        'rg_w_out': nrm(ks[22], (N_RG_LAYERS, D_RNN, D_MODEL), D_RNN ** -0.5),
        'final_g': 1.0 + nrm(ks[23], (D_MODEL,), 0.02),
    }


def reference(x_prompt, x_sample, c_prompt, c_sample, cache_k, cache_v, page_table, state_conv, state_h,
              norm_g, mod_w, mod_b, att_w_in, att_bias, att_w_out, rg_w_in, rg_conv_w, rg_conv_b, rg_w_a, rg_b_a,
              rg_w_x, rg_b_x, rg_lambda, rg_w_out, final_g):
    xp, xs = x_prompt, x_sample
    kp_l, vp_l, ks_l, vs_l = [], [], [], []
    cp_l, hp_l, cs_l, hs_l = [], [], [], []
    n_seq, n_pages = page_table.shape
    for i in range(DEPTH):
        hp, gp = _modulated_norm(xp, c_prompt, norm_g[i], mod_w[i], mod_b[i])
        hs, gs = _modulated_norm(xs, c_sample, norm_g[i], mod_w[i], mod_b[i])
        j = i // N_MIXERS
        if i % N_MIXERS == 0:
            q, k, v, g = _att_project(hp, att_w_in[j])
            out_p = _att_out(_sb_prompt(q, k, v, att_bias[j]), g, att_w_out[j])
            kp_l.append(k)
            vp_l.append(v)
            q, k, v, g = _att_project(hs, att_w_in[j])
            past_shape = (n_seq, n_pages * PAGE_SIZE, SB_HEADS, SB_HEAD_DIM)
            k_past = cache_k[j][page_table].reshape(past_shape)
            v_past = cache_v[j][page_table].reshape(past_shape)
            out_s = _att_out(_sb_sample(q, k, v, k_past, v_past, att_bias[j]), g, att_w_out[j])
            ks_l.append(k)
            vs_l.append(v)
        else:
            zero_buf = jnp.zeros((xp.shape[0], CONV_W - 1, D_RNN), xp.dtype)
            zero_h = jnp.zeros((xp.shape[0], D_RNN), xp.dtype)
            out_p, ctail, hlast = _rg_branch(hp, zero_buf, zero_h, rg_w_in[j], rg_conv_w[j], rg_conv_b[j],
                                             rg_w_a[j], rg_b_a[j], rg_w_x[j], rg_b_x[j], rg_lambda[j], rg_w_out[j])
            cp_l.append(ctail)
            hp_l.append(hlast)
            out_s, ctail, hlast = _rg_branch(hs, state_conv[j], state_h[j], rg_w_in[j], rg_conv_w[j], rg_conv_b[j],
                                             rg_w_a[j], rg_b_a[j], rg_w_x[j], rg_b_x[j], rg_lambda[j], rg_w_out[j])
            cs_l.append(ctail)
            hs_l.append(hlast)
        xp = xp + gp * out_p
        xs = xs + gs * out_s
    y_prompt = _rmsnorm(xp, final_g)
    y_sample = _rmsnorm(xs, final_g)
    return (y_prompt, y_sample, jnp.stack(kp_l), jnp.stack(vp_l), jnp.stack(ks_l), jnp.stack(vs_l),
            jnp.stack(cp_l), jnp.stack(hp_l), jnp.stack(cs_l), jnp.stack(hs_l))
```

```python
import jax
import jax.numpy as jnp
from jax import lax
from jax.experimental import pallas as pl
from jax.experimental.pallas import tpu as pltpu

F32 = jnp.float32
BF16 = jnp.bfloat16

D_MODEL = 1024
HEADS = 16
HEAD_DIM = 64
LANES = 128
SUBLANES = 8
HEADS_PER_LANE_TILE = LANES // HEAD_DIM
PAGE = 128
RG_BLOCKS = 4
RG_BLOCK_W = D_MODEL // RG_BLOCKS
CONV_W = 4
RG_C = 8.0
EPS = 1e-6
Q_SCALE = HEAD_DIM ** -0.5

ROW_TILE = 512
ATT_TILE = 256
RG_TILE = 256
PAGES_PER_STEP = 8
VMEM_LIMIT = 56 * 1024 * 1024

NT_DIMS = (((1,), (1,)), ((), ()))


def _dot(a, b):
    return jnp.dot(a, b, preferred_element_type=F32)


def _dot_nt(a, b):
    return lax.dot_general(a, b, NT_DIMS, preferred_element_type=F32)


def _neg_softplus(z):
    return -(jnp.maximum(z, 0.0) + jnp.log(1.0 + jnp.exp(-jnp.abs(z))))


def _split2(x):
    hi = x.astype(BF16)
    lo = (x - hi.astype(F32)).astype(BF16)
    return hi, lo


def _split3(x):
    hi = x.astype(BF16)
    r = x - hi.astype(F32)
    mid = r.astype(BF16)
    lo = (r - mid.astype(F32)).astype(BF16)
    return hi, mid, lo


def _dot3(x, m):
    hi, mid, lo = _split3(x)
    return _dot(hi, m) + _dot(mid, m) + _dot(lo, m)


def _rmsnorm(x, g):
    ms = jnp.mean(x * x, axis=-1, keepdims=True)
    return (x * lax.rsqrt(ms + EPS)) * g


def _params(semantics):
    return pltpu.CompilerParams(dimension_semantics=semantics, vmem_limit_bytes=VMEM_LIMIT)


def _resident(shape):
    zeros = (0,) * len(shape)
    return pl.BlockSpec(shape, lambda *_: zeros, pipeline_mode=pl.Buffered(1))


def _mod_kernel(c_ref, w_ref, b_ref, o_ref):
    c = c_ref[...]
    o_ref[...] = _dot(jax.nn.silu(c).astype(BF16), w_ref[...].astype(BF16)) + b_ref[...]


def _modulation(c_all, mod_w, mod_b):
    depth, d, n = mod_w.shape
    rows = c_all.shape[0]
    tn = 1024
    return pl.pallas_call(
        _mod_kernel,
        out_shape=jax.ShapeDtypeStruct((depth, rows, n), F32),
        grid=(depth, n // tn),
        in_specs=[
            pl.BlockSpec((rows, d), lambda l, j: (0, 0)),
            pl.BlockSpec((None, d, tn), lambda l, j: (l, 0, j)),
            pl.BlockSpec((None, 1, tn), lambda l, j: (l, 0, j)),
        ],
        out_specs=pl.BlockSpec((None, rows, tn), lambda l, j: (l, 0, j)),
        compiler_params=_params(("arbitrary", "arbitrary")),
        name="modulation",
    )(c_all, mod_w, mod_b.reshape(depth, 1, n))


def _att_in_hidden(x_ref, shift_ref, scale_ref, g_ref):
    h = _rmsnorm(x_ref[...], g_ref[...]) * (1.0 + scale_ref[...]) + shift_ref[...]
    return h.astype(BF16)


def _att_in_prompt_kernel(x_ref, shift_ref, scale_ref, g_ref, w_ref, wkv_t_ref,
                          q_ref, kt_ref, vt_ref, gate_ref, kbt_ref, vb_ref):
    d = D_MODEL
    t = ATT_TILE
    h = _att_in_hidden(x_ref, shift_ref, scale_ref, g_ref)
    q_ref[...] = (_dot(h, w_ref[:, 0:d]) * Q_SCALE).astype(BF16)
    kt = _dot_nt(wkv_t_ref[0:d, :], h)
    kt_ref[...] = kt
    for j in range(kbt_ref.shape[0]):
        kbt_ref[j] = kt[:, j * t:(j + 1) * t].astype(BF16)
    vt_ref[...] = _dot_nt(wkv_t_ref[d:2 * d, :], h)
    vb_ref[...] = _dot(h, w_ref[:, 2 * d:3 * d]).astype(BF16)
    gate_ref[...] = _dot(h, w_ref[:, 3 * d:4 * d])


def _att_in_prompt(x, shift, scale, g, w, wkv_t, *, batch, seq):
    r, d = x.shape
    tm = ROW_TILE
    t = ATT_TILE
    per_b = seq // tm
    row = pl.BlockSpec((tm, d), lambda i: (i, 0))
    mod = pl.BlockSpec((None, 1, d), lambda i: (i // per_b, 0, 0))
    feat = pl.BlockSpec((None, d, tm), lambda i: (i // per_b, 0, i % per_b))
    feat_tiles = pl.BlockSpec((None, tm // t, d, t), lambda i: (i // per_b, i % per_b, 0, 0))
    return pl.pallas_call(
        _att_in_prompt_kernel,
        out_shape=(
            jax.ShapeDtypeStruct((r, d), BF16),
            jax.ShapeDtypeStruct((batch, d, seq), F32),
            jax.ShapeDtypeStruct((batch, d, seq), F32),
            jax.ShapeDtypeStruct((r, d), F32),
            jax.ShapeDtypeStruct((batch, seq // t, d, t), BF16),
            jax.ShapeDtypeStruct((r, d), BF16),
        ),
        grid=(r // tm,),
        in_specs=[row, mod, mod, _resident((1, d)), _resident((d, 4 * d)), _resident((2 * d, d))],
        out_specs=(row, feat, feat, row, feat_tiles, row),
        compiler_params=_params(("arbitrary",)),
        name="att_in_prompt",
    )(x, shift, scale, g, w, wkv_t)


def _att_in_sample_kernel(x_ref, shift_ref, scale_ref, g_ref, w_ref, q_ref, k_ref, v_ref, gate_ref):
    d = D_MODEL
    h = _att_in_hidden(x_ref, shift_ref, scale_ref, g_ref)
    q_ref[...] = (_dot(h, w_ref[:, 0:d]) * Q_SCALE).astype(BF16)
    k_ref[...] = _dot(h, w_ref[:, d:2 * d])
    v_ref[...] = _dot(h, w_ref[:, 2 * d:3 * d])
    gate_ref[...] = _dot(h, w_ref[:, 3 * d:4 * d])


def _att_in_sample(x, shift, scale, g, w):
    r, d = x.shape
    sds = jax.ShapeDtypeStruct((r, d), F32)
    return pl.pallas_call(
        _att_in_sample_kernel,
        out_shape=(jax.ShapeDtypeStruct((r, d), BF16), sds, sds, sds),
        compiler_params=pltpu.CompilerParams(vmem_limit_bytes=VMEM_LIMIT),
        name="att_in_sample",
    )(x, shift, scale, g, w)


def _sb_prompt_kernel(bias_ref, q_ref, kt_ref, v_ref, u_ref, o_ref):
    t = ATT_TILE
    hp = pl.program_id(1)
    qi = pl.program_id(2)
    lane = lax.broadcasted_iota(jnp.int32, (1, LANES), 1)
    row = lax.broadcasted_iota(jnp.int32, (t, t), 0)
    col = lax.broadcasted_iota(jnp.int32, (t, t), 1)
    causal = col < row
    q = q_ref[...].astype(F32)
    u = u_ref[...]

    def tile(j, carry, acc, qh, bias, mask):
        v = v_ref[pl.ds(pl.multiple_of(j * t, t), t), :]
        z = _dot(qh, kt_ref[j]) + bias
        lk = _neg_softplus(z)
        if mask is not None:
            lk = jnp.where(mask, lk, 0.0)
        hi, lo = _split2(lk)
        later = _dot(hi, u) + _dot(lo, u)
        a = jnp.exp(z + lk + later + carry)
        if mask is not None:
            a = jnp.where(mask, a, 0.0)
        acc = acc + _dot(a.astype(BF16), v)
        carry = carry + jnp.sum(lk, axis=-1, keepdims=True)
        return carry, acc

    outs = []
    for hh in range(HEADS_PER_LANE_TILE):
        in_head = (lane >= hh * HEAD_DIM) & (lane < (hh + 1) * HEAD_DIM)
        qh = jnp.where(in_head, q, 0.0).astype(BF16)
        bias = bias_ref[hp * HEADS_PER_LANE_TILE + hh]
        carry = jnp.zeros((t, 1), F32)
        acc = jnp.zeros((t, LANES), F32)
        carry, acc = tile(qi, carry, acc, qh, bias, causal)

        def body(it, ca, qh=qh, bias=bias):
            return tile(qi - 1 - it, ca[0], ca[1], qh, bias, None)

        carry, acc = lax.fori_loop(0, qi, body, (carry, acc))
        outs.append(acc)
    o_ref[...] = jnp.where(lane < HEAD_DIM, outs[0], outs[1])


def _sb_prompt(bias, q, kbt, vb, u, *, batch, seq):
    d = q.shape[1]
    t = ATT_TILE
    nq = seq // t
    return pl.pallas_call(
        _sb_prompt_kernel,
        out_shape=jax.ShapeDtypeStruct((batch * seq, d), F32),
        grid=(batch, d // LANES, nq),
        in_specs=[
            pl.BlockSpec(memory_space=pltpu.SMEM),
            pl.BlockSpec((t, LANES), lambda b, h, i: (b * nq + i, h)),
            pl.BlockSpec((None, nq, LANES, t), lambda b, h, i: (b, 0, h, 0)),
            pl.BlockSpec((seq, LANES), lambda b, h, i: (b, h)),
            pl.BlockSpec((t, t), lambda b, h, i: (0, 0)),
        ],
        out_specs=pl.BlockSpec((t, LANES), lambda b, h, i: (b * nq + i, h)),
        compiler_params=_params(("arbitrary", "arbitrary", "arbitrary")),
        name="sb_prompt",
    )(bias, q, kbt, vb, u)


def _sb_sample_kernel(pt_ref, q_ref, kn_ref, vn_ref, bias_ref, bias_row_ref, u_ref, ones_ref, hsel_ref,
                      *rest):
    g_n = PAGES_PER_STEP
    k_refs = rest[:g_n]
    v_refs = rest[g_n:2 * g_n]
    o_ref = rest[2 * g_n]
    qb_ref, acc_ref, carry_ref = rest[2 * g_n + 1:]
    c = pl.program_id(1)
    nc = pl.num_programs(1)
    d = D_MODEL
    ones = ones_ref[...]

    def new_token():
        mask_new = jnp.zeros((SUBLANES, LANES), jnp.int32) < jnp.zeros((SUBLANES, LANES), jnp.int32)
        qk = q_ref[...].astype(F32) * kn_ref[...].astype(BF16).astype(F32)
        z_new = _dot3(jnp.broadcast_to(qk, (SUBLANES, d)), hsel_ref[...]) + bias_row_ref[...]
        lk_new = _neg_softplus(z_new)
        a_new = jnp.where(mask_new, jnp.exp(z_new + lk_new), 0.0)
        return a_new, jnp.where(mask_new, lk_new, 0.0)

    @pl.when(c == 0)
    def _():
        blk = 256
        eye = (lax.broadcasted_iota(jnp.int32, (blk, blk), 0)
               == lax.broadcasted_iota(jnp.int32, (blk, blk), 1))
        for rb in range(d // blk):
            qrow = q_ref[:, rb * blk:(rb + 1) * blk].astype(F32)
            dq = jnp.where(eye, jnp.broadcast_to(qrow, (blk, blk)), 0.0).astype(BF16)
            qb_ref[rb * blk:(rb + 1) * blk, :] = _dot(dq, jnp.ones((blk, LANES), BF16))
        acc_ref[...] = jnp.zeros((d, PAGE), F32)
        _, lk_new = new_token()
        head_diag = (lax.broadcasted_iota(jnp.int32, (HEADS, LANES), 0)
                     == lax.broadcasted_iota(jnp.int32, (HEADS, LANES), 1))
        carry_ref[...] = _dot3(jnp.where(head_diag, jnp.broadcast_to(lk_new[0:1, :], (HEADS, LANES)), 0.0), ones)

    qb = qb_ref[...].reshape(HEADS, HEAD_DIM, PAGE)
    zs = [jnp.sum(k_refs[g][...].reshape(HEADS, HEAD_DIM, PAGE) * qb, axis=1) + bias_ref[...]
          for g in range(g_n)]
    z = jnp.concatenate(zs, axis=0)
    lk = _neg_softplus(z)
    hi, mid, lo = _split3(lk)
    u = u_ref[...]
    later = _dot(hi, u) + _dot(mid, u) + _dot(lo, u)
    tot = _dot(hi, ones) + _dot(mid, ones) + _dot(lo, ones)
    ex = z + lk + later
    off = carry_ref[...]
    a = [None] * g_n
    for g in reversed(range(g_n)):
        sl = slice(g * HEADS, (g + 1) * HEADS)
        a[g] = jnp.exp(ex[sl] + off)
        off = off + tot[sl]
    carry_ref[...] = off
    acc = acc_ref[...].reshape(HEADS, HEAD_DIM, PAGE)
    for g in range(g_n):
        acc = acc + a[g][:, None, :] * v_refs[g][...].reshape(HEADS, HEAD_DIM, PAGE)
    acc_ref[...] = acc.reshape(d, PAGE)

    @pl.when(c == nc - 1)
    def _():
        hi_a, mid_a, lo_a = _split3(acc.reshape(d, PAGE))
        ones8 = jnp.ones((SUBLANES, PAGE), BF16)
        o = _dot_nt(ones8, hi_a) + _dot_nt(ones8, mid_a) + _dot_nt(ones8, lo_a)
        a_new, _ = new_token()
        o_new = _dot_nt(a_new.astype(BF16), hsel_ref[...]) * vn_ref[...]
        o_ref[...] = o[0:1, :] + o_new[0:1, :]


def _sb_sample(page_table, q, k_new, v_new, bias, cache_kt, cache_vt):
    nb, d = q.shape
    n_pages = page_table.shape[1]
    g_n = PAGES_PER_STEP
    nc = n_pages // g_n
    i32 = jnp.int32

    key = jnp.arange(PAGE, dtype=i32)
    u = (key[:, None] > key[None, :]).astype(BF16)
    ones = jnp.ones((LANES, LANES), BF16)
    feat_head = jnp.arange(d, dtype=i32) // HEAD_DIM
    hsel = (feat_head[:, None] == jnp.arange(LANES, dtype=i32)[None, :]).astype(BF16)
    bias_col = jnp.broadcast_to(bias[:, None], (HEADS, PAGE))
    bias_row = jnp.pad(bias, (0, LANES - HEADS)).reshape(1, LANES)

    def page_spec(g):
        return pl.BlockSpec(
            (None, d, PAGE),
            lambda b, c, pt, g=g: (pt[b, (nc - 1 - c) * g_n + g], 0, 0))

    const2 = lambda b, c, pt: (0, 0)
    row3 = pl.BlockSpec((None, 1, d), lambda b, c, pt: (b, 0, 0))
    grid_spec = pltpu.PrefetchScalarGridSpec(
        num_scalar_prefetch=1,
        grid=(nb, nc),
        in_specs=[row3, row3, row3,
                  pl.BlockSpec((HEADS, PAGE), const2),
                  pl.BlockSpec((1, LANES), const2),
                  pl.BlockSpec((PAGE, PAGE), const2),
                  pl.BlockSpec((LANES, LANES), const2),
                  pl.BlockSpec((d, LANES), const2)]
                 + [page_spec(g) for g in range(g_n)] * 2,
        out_specs=row3,
        scratch_shapes=[pltpu.VMEM((d, LANES), F32),
                        pltpu.VMEM((d, PAGE), F32),
                        pltpu.VMEM((HEADS, PAGE), F32)],
    )
    out = pl.pallas_call(
        _sb_sample_kernel,
        out_shape=jax.ShapeDtypeStruct((nb, 1, d), F32),
        grid_spec=grid_spec,
        compiler_params=_params(("arbitrary", "arbitrary")),
        name="sb_sample",
    )(page_table, q.reshape(nb, 1, d), k_new.reshape(nb, 1, d), v_new.reshape(nb, 1, d),
      bias_col, bias_row, u, ones, hsel, *([cache_kt] * g_n), *([cache_vt] * g_n))
    return out.reshape(nb, d)


def _att_out_rg_in_kernel(o_ref, gate_ref, x_ref, mgate_ref, shift_ref, scale_ref, g_ref,
                          wout_ref, win_ref, x1_ref, xbr_ref, gbr_ref):
    y = _dot((o_ref[...] * jax.nn.silu(gate_ref[...])).astype(BF16), wout_ref[...])
    x1 = x_ref[...] + mgate_ref[...] * y
    x1_ref[...] = x1
    h = (_rmsnorm(x1, g_ref[...]) * (1.0 + scale_ref[...]) + shift_ref[...]).astype(BF16)
    d = D_MODEL
    xbr_ref[...] = _dot(h, win_ref[:, 0:d])
    gbr_ref[...] = _dot(h, win_ref[:, d:2 * d])


def _att_out_rg_in(o, gate, x, mgate, shift, scale, g, w_out, w_in, *, tm, rows_per_mod):
    r, d = x.shape
    mod_rows = shift.shape[1]
    row = pl.BlockSpec((tm, d), lambda i: (i, 0))
    mod = pl.BlockSpec((None, mod_rows, d), lambda i: ((i * tm) // rows_per_mod, 0, 0))
    sds = jax.ShapeDtypeStruct((r, d), F32)
    return pl.pallas_call(
        _att_out_rg_in_kernel,
        out_shape=(sds, sds, sds),
        grid=(r // tm,),
        in_specs=[row, row, row, mod, mod, mod,
                  _resident((1, d)), _resident((d, d)), _resident((d, 2 * d))],
        out_specs=(row, row, row),
        compiler_params=_params(("arbitrary",)),
        name="att_out_rg_in",
    )(o, gate, x, mgate, shift, scale, g, w_out, w_in)


def _rg_gates(xc, wa_ref, ba_ref, wx_ref, bx_ref, lam_ref):
    xcb = xc.astype(BF16)
    w = RG_BLOCK_W
    ra = jnp.concatenate([_dot(xcb[:, n * w:(n + 1) * w], wa_ref[n]) for n in range(RG_BLOCKS)], axis=-1)
    rx = jnp.concatenate([_dot(xcb[:, n * w:(n + 1) * w], wx_ref[n]) for n in range(RG_BLOCKS)], axis=-1)
    r = jax.nn.sigmoid(ra + ba_ref[...])
    ig = jax.nn.sigmoid(rx + bx_ref[...])
    lam = lam_ref[...]
    log_sig_lam = -(jnp.maximum(-lam, 0.0) + jnp.log1p(jnp.exp(-jnp.abs(lam))))
    log_a = RG_C * r * log_sig_lam
    a = jnp.exp(log_a)
    u = jnp.sqrt(-jnp.tanh(log_a) * (a * a + 1.0)) * (ig * xc)
    return a, u


def _rg_finish(y, gbr, x1, mgate, wout_ref, fg_ref):
    out = _dot((y * jax.nn.silu(gbr)).astype(BF16), wout_ref[...])
    return _rmsnorm(x1 + mgate * out, fg_ref[...])


def _rg_prompt_kernel(xbr_ref, gbr_ref, x1_ref, mgate_ref, cw_ref, cb_ref, wa_ref, ba_ref,
                      wx_ref, bx_ref, lam_ref, wout_ref, fg_ref,
                      y_ref, tail_ref, hlast_ref, xx_s, a_s, u_s, h_s):
    tt = RG_TILE
    d = D_MODEL
    ti = pl.program_id(1)
    pad = SUBLANES

    @pl.when(ti == 0)
    def _():
        xx_s[0:pad, :] = jnp.zeros((pad, d), F32)
        h_s[...] = jnp.zeros((SUBLANES, d), F32)

    x = xbr_ref[...]
    xx_s[pad:pad + tt, :] = x
    xc = cb_ref[...] + cw_ref[0:1, :] * xx_s[pad - 3:pad - 3 + tt, :]
    xc = xc + cw_ref[1:2, :] * xx_s[pad - 2:pad - 2 + tt, :]
    xc = xc + cw_ref[2:3, :] * xx_s[pad - 1:pad - 1 + tt, :]
    xc = xc + cw_ref[3:4, :] * x
    xx_s[0:pad, :] = x[tt - pad:tt, :]

    a, u = _rg_gates(xc, wa_ref, ba_ref, wx_ref, bx_ref, lam_ref)
    a_s[...] = a
    u_s[...] = u
    sub = lax.broadcasted_iota(jnp.int32, (SUBLANES, d), 0)

    def group(i, h):
        r0 = pl.multiple_of(i * SUBLANES, SUBLANES)
        ag = a_s[pl.ds(r0, SUBLANES), :]
        ug = u_s[pl.ds(r0, SUBLANES), :]
        for s in (1, 2, 4):
            a_prev = jnp.where(sub >= s, pltpu.roll(ag, s, axis=0), 1.0)
            u_prev = jnp.where(sub >= s, pltpu.roll(ug, s, axis=0), 0.0)
            ug = ag * u_prev + ug
            ag = ag * a_prev
        hg = ag * h + ug
        u_s[pl.ds(r0, SUBLANES), :] = hg
        return jnp.broadcast_to(hg[SUBLANES - 1:SUBLANES, :], (SUBLANES, d))

    h = lax.fori_loop(0, tt // SUBLANES, group, h_s[...], unroll=4)
    h_s[...] = h

    y_ref[...] = _rg_finish(u_s[...], gbr_ref[...], x1_ref[...], mgate_ref[...], wout_ref, fg_ref)

    @pl.when(ti == pl.num_programs(1) - 1)
    def _():
        tail_ref[...] = x[tt - (CONV_W - 1):tt, :]
        hlast_ref[...] = h[0:1, :]


def _rg_prompt(xbr, gbr, x1, mgate, cw, cb, wa, ba, wx, bx, lam, wout, fg, *, batch, seq):
    d = D_MODEL
    tt = RG_TILE
    nt = seq // tt
    row = pl.BlockSpec((tt, d), lambda b, t: (b * nt + t, 0))
    vec = _resident((1, d))
    gates_w = _resident((RG_BLOCKS, RG_BLOCK_W, RG_BLOCK_W))
    return pl.pallas_call(
        _rg_prompt_kernel,
        out_shape=(jax.ShapeDtypeStruct((batch * seq, d), F32),
                   jax.ShapeDtypeStruct((batch, CONV_W - 1, d), F32),
                   jax.ShapeDtypeStruct((batch, 1, d), F32)),
        grid=(batch, nt),
        in_specs=[row, row, row,
                  pl.BlockSpec((None, 1, d), lambda b, t: (b, 0, 0)),
                  _resident((CONV_W, d)),
                  vec, gates_w, vec, gates_w, vec, vec,
                  _resident((d, d)),
                  vec],
        out_specs=(row,
                   pl.BlockSpec((None, CONV_W - 1, d), lambda b, t: (b, 0, 0)),
                   pl.BlockSpec((None, 1, d), lambda b, t: (b, 0, 0))),
        scratch_shapes=[pltpu.VMEM((tt + SUBLANES, d), F32),
                        pltpu.VMEM((tt, d), F32),
                        pltpu.VMEM((tt, d), F32),
                        pltpu.VMEM((SUBLANES, d), F32)],
        compiler_params=_params(("arbitrary", "arbitrary")),
        name="rg_prompt",
    )(xbr, gbr, x1, mgate, cw, cb, wa, ba, wx, bx, lam, wout, fg)


def _rg_sample_kernel(xbr_ref, gbr_ref, x1_ref, mgate_ref, sconv_ref, sh_ref, cw_ref, cb_ref,
                      wa_ref, ba_ref, wx_ref, bx_ref, lam_ref, wout_ref, fg_ref,
                      y_ref, tail_ref, h_ref):
    x = xbr_ref[...]
    xc = cb_ref[...] + cw_ref[0:1, :] * sconv_ref[0]
    xc = xc + cw_ref[1:2, :] * sconv_ref[1]
    xc = xc + cw_ref[2:3, :] * sconv_ref[2]
    xc = xc + cw_ref[3:4, :] * x
    a, u = _rg_gates(xc, wa_ref, ba_ref, wx_ref, bx_ref, lam_ref)
    h = a * sh_ref[...] + u
    h_ref[...] = h
    tail_ref[0] = sconv_ref[1]
    tail_ref[1] = sconv_ref[2]
    tail_ref[2] = x
    y_ref[...] = _rg_finish(h, gbr_ref[...], x1_ref[...], mgate_ref[...], wout_ref, fg_ref)


def _rg_sample(xbr, gbr, x1, mgate, sconv, sh, cw, cb, wa, ba, wx, bx, lam, wout, fg):
    nb, d = xbr.shape
    sds = jax.ShapeDtypeStruct((nb, d), F32)
    return pl.pallas_call(
        _rg_sample_kernel,
        out_shape=(sds, jax.ShapeDtypeStruct((CONV_W - 1, nb, d), F32), sds),
        compiler_params=pltpu.CompilerParams(vmem_limit_bytes=VMEM_LIMIT),
        name="rg_sample",
    )(xbr, gbr, x1, mgate, sconv, sh, cw, cb, wa, ba, wx, bx, lam, wout, fg)


def kernel(x_prompt, x_sample, c_prompt, c_sample, cache_k, cache_v, page_table, state_conv, state_h,
           norm_g, mod_w, mod_b, att_w_in, att_bias, att_w_out, rg_w_in, rg_conv_w, rg_conv_b,
           rg_w_a, rg_b_a, rg_w_x, rg_b_x, rg_lambda, rg_w_out, final_g):
    batch, seq, d = x_prompt.shape
    nb, dec_seq, _ = x_sample.shape
    assert d == D_MODEL and dec_seq == 1 and mod_w.shape[0] == 2

    mod = _modulation(jnp.concatenate([c_prompt, c_sample], axis=0), mod_w, mod_b)

    def mod_parts(layer, lo, hi, shape):
        m = mod[layer, lo:hi]
        return tuple(m[:, i * d:(i + 1) * d].reshape(shape) for i in range(3))

    shift0p, scale0p, gate0p = mod_parts(0, 0, batch, (batch, 1, d))
    shift1p, scale1p, gate1p = mod_parts(1, 0, batch, (batch, 1, d))
    shift0s, scale0s, gate0s = mod_parts(0, batch, batch + nb, (nb, d))
    shift1s, scale1s, gate1s = mod_parts(1, batch, batch + nb, (nb, d))

    g0, g1 = norm_g[0].reshape(1, d), norm_g[1].reshape(1, d)
    w_att_in = att_w_in[0].astype(BF16)
    w_kv_t = jnp.swapaxes(att_w_in[0][:, d:3 * d], 0, 1).astype(BF16)
    w_att_out = att_w_out[0].astype(BF16)
    w_rg_in = rg_w_in[0].astype(BF16)
    w_rg_out = rg_w_out[0].astype(BF16)
    wa, wx = rg_w_a[0].astype(BF16), rg_w_x[0].astype(BF16)
    vec = lambda v: v.reshape(1, d)
    rg_consts = (rg_conv_w[0], vec(rg_conv_b[0]), wa, vec(rg_b_a[0]), wx, vec(rg_b_x[0]),
                 vec(rg_lambda[0]), w_rg_out, vec(final_g))
    bias = att_bias[0]

    xp = x_prompt.reshape(batch * seq, d)
    q_p, kt_p, vt_p, gate_p, kbt_p, vb_p = _att_in_prompt(xp, shift0p, scale0p, g0, w_att_in, w_kv_t,
                                                          batch=batch, seq=seq)
    idx = jnp.arange(ATT_TILE, dtype=jnp.int32)
    u = (idx[:, None] > idx[None, :]).astype(BF16)
    o_p = _sb_prompt(bias, q_p, kbt_p, vb_p, u, batch=batch, seq=seq)
    x1_p, xbr_p, gbr_p = _att_out_rg_in(o_p, gate_p, xp, gate0p, shift1p, scale1p, g1, w_att_out, w_rg_in,
                                        tm=ROW_TILE, rows_per_mod=seq)
    y_p, tail_p, hlast_p = _rg_prompt(xbr_p, gbr_p, x1_p, gate1p, *rg_consts, batch=batch, seq=seq)

    xs = x_sample.reshape(nb, d)
    q_s, k_s, v_s, gate_s = _att_in_sample(xs, shift0s, scale0s, g0, w_att_in)
    pages_t = lambda cache: jnp.transpose(cache, (0, 1, 3, 4, 2)).reshape(-1, d, PAGE)
    o_s = _sb_sample(page_table, q_s, k_s, v_s, bias, pages_t(cache_k), pages_t(cache_v))
    as3 = lambda m: m.reshape(1, nb, d)
    x1_s, xbr_s, gbr_s = _att_out_rg_in(o_s, gate_s, xs, as3(gate0s), as3(shift1s), as3(scale1s), g1,
                                        w_att_out, w_rg_in, tm=nb, rows_per_mod=nb)
    y_s, tail_s, h_s = _rg_sample(xbr_s, gbr_s, x1_s, gate1s,
                                  jnp.swapaxes(state_conv[0], 0, 1), state_h[0], *rg_consts)

    def kv_rows(t):
        return jnp.transpose(t.reshape(1, batch, HEADS, HEAD_DIM, seq), (0, 1, 4, 2, 3))

    kv_s = (1, nb, 1, HEADS, HEAD_DIM)
    return (y_p.reshape(batch, seq, d), y_s.reshape(nb, 1, d),
            kv_rows(kt_p), kv_rows(vt_p), k_s.reshape(kv_s), v_s.reshape(kv_s),
            tail_p.reshape(1, batch, CONV_W - 1, d), hlast_p.reshape(1, batch, d),
            jnp.swapaxes(tail_s, 0, 1).reshape(1, nb, CONV_W - 1, d), h_s.reshape(1, nb, d))
```

```python
import jax
import jax.numpy as jnp
from jax import lax
from jax.experimental import pallas as pl
from jax.experimental.pallas import tpu as pltpu

F32 = jnp.float32
BF16 = jnp.bfloat16

D_MODEL = 1024
HEADS = 16
HEAD_DIM = 64
LANES = 128
SUBLANES = 8
HEADS_PER_LANE_TILE = LANES // HEAD_DIM
PAGE = 128
RG_BLOCKS = 4
RG_BLOCK_W = D_MODEL // RG_BLOCKS
CONV_W = 4
RG_C = 8.0
EPS = 1e-6
Q_SCALE = HEAD_DIM ** -0.5
LOG2_E = 1.4426950408889634
F32_TINY = 1.1754943508222875e-38

ROW_TILE = 512
ATT_TILE = 256
RG_TILE = 256
PAGES_PER_STEP = 8
VMEM_LIMIT = 56 * 1024 * 1024

NT_DIMS = (((1,), (1,)), ((), ()))


def _dot(a, b):
    return jnp.dot(a, b, preferred_element_type=F32)


def _dot_nt(a, b):
    return lax.dot_general(a, b, NT_DIMS, preferred_element_type=F32)


def _sigmoid(x):
    return 0.5 * jnp.tanh(0.5 * x) + 0.5


def _silu(x):
    return x * _sigmoid(x)


def _neg_softplus(z):
    return -(jnp.maximum(z, 0.0) + jnp.log(1.0 + jnp.exp(-jnp.abs(z))))


def _split2(x):
    hi = x.astype(BF16)
    lo = (x - hi.astype(F32)).astype(BF16)
    return hi, lo


def _split3(x):
    hi = x.astype(BF16)
    r = x - hi.astype(F32)
    mid = r.astype(BF16)
    lo = (r - mid.astype(F32)).astype(BF16)
    return hi, mid, lo


def _dot3(x, m):
    hi, mid, lo = _split3(x)
    return _dot(hi, m) + _dot(mid, m) + _dot(lo, m)


def _rmsnorm(x, g):
    ms = jnp.mean(x * x, axis=-1, keepdims=True)
    return (x * lax.rsqrt(ms + EPS)) * g


def _params(semantics):
    return pltpu.CompilerParams(dimension_semantics=semantics, vmem_limit_bytes=VMEM_LIMIT)


def _resident(shape):
    zeros = (0,) * len(shape)
    return pl.BlockSpec(shape, lambda *_: zeros, pipeline_mode=pl.Buffered(1))


def _mod_kernel(c_ref, w_ref, b_ref, o_ref):
    c = c_ref[...]
    o_ref[...] = _dot(_silu(c).astype(BF16), w_ref[...].astype(BF16)) + b_ref[...]


def _modulation(c_all, mod_w, mod_b):
    depth, d, n = mod_w.shape
    rows = c_all.shape[0]
    tn = 1024
    return pl.pallas_call(
        _mod_kernel,
        out_shape=jax.ShapeDtypeStruct((depth, rows, n), F32),
        grid=(depth, n // tn),
        in_specs=[
            pl.BlockSpec((rows, d), lambda l, j: (0, 0)),
            pl.BlockSpec((None, d, tn), lambda l, j: (l, 0, j)),
            pl.BlockSpec((None, 1, tn), lambda l, j: (l, 0, j)),
        ],
        out_specs=pl.BlockSpec((None, rows, tn), lambda l, j: (l, 0, j)),
        compiler_params=_params(("arbitrary", "arbitrary")),
        name="modulation",
    )(c_all, mod_w, mod_b.reshape(depth, 1, n))


def _att_in_hidden(x_ref, shift_ref, scale_ref, g_ref):
    h = _rmsnorm(x_ref[...], g_ref[...]) * (1.0 + scale_ref[...]) + shift_ref[...]
    return h.astype(BF16)


def _att_in_prompt_kernel(x_ref, shift_ref, scale_ref, g_ref, w_ref, wkv_t_ref,
                          q_ref, kt_ref, vt_ref, gate_ref, kbt_ref, vb_ref):
    d = D_MODEL
    t = ATT_TILE
    h = _att_in_hidden(x_ref, shift_ref, scale_ref, g_ref)
    q_ref[...] = (_dot(h, w_ref[:, 0:d]) * Q_SCALE).astype(BF16)
    kt = _dot_nt(wkv_t_ref[0:d, :], h)
    kt_ref[...] = kt
    for j in range(kbt_ref.shape[0]):
        kbt_ref[j] = kt[:, j * t:(j + 1) * t].astype(BF16)
    vt_ref[...] = _dot_nt(wkv_t_ref[d:2 * d, :], h)
    vb_ref[...] = _dot(h, w_ref[:, 2 * d:3 * d]).astype(BF16)
    gate_ref[...] = _dot(h, w_ref[:, 3 * d:4 * d])


def _att_in_prompt(x, shift, scale, g, w, wkv_t, *, batch, seq):
    r, d = x.shape
    tm = ROW_TILE
    t = ATT_TILE
    per_b = seq // tm
    row = pl.BlockSpec((tm, d), lambda i: (i, 0))
    mod = pl.BlockSpec((None, 1, d), lambda i: (i // per_b, 0, 0))
    feat = pl.BlockSpec((None, d, tm), lambda i: (i // per_b, 0, i % per_b))
    feat_tiles = pl.BlockSpec((None, tm // t, d, t), lambda i: (i // per_b, i % per_b, 0, 0))
    return pl.pallas_call(
        _att_in_prompt_kernel,
        out_shape=(
            jax.ShapeDtypeStruct((r, d), BF16),
            jax.ShapeDtypeStruct((batch, d, seq), F32),
            jax.ShapeDtypeStruct((batch, d, seq), F32),
            jax.ShapeDtypeStruct((r, d), F32),
            jax.ShapeDtypeStruct((batch, seq // t, d, t), BF16),
            jax.ShapeDtypeStruct((r, d), BF16),
        ),
        grid=(r // tm,),
        in_specs=[row, mod, mod, _resident((1, d)), _resident((d, 4 * d)), _resident((2 * d, d))],
        out_specs=(row, feat, feat, row, feat_tiles, row),
        compiler_params=_params(("arbitrary",)),
        name="att_in_prompt",
    )(x, shift, scale, g, w, wkv_t)


def _att_in_sample_kernel(x_ref, shift_ref, scale_ref, g_ref, w_ref, q_ref, k_ref, v_ref, gate_ref):
    d = D_MODEL
    h = _att_in_hidden(x_ref, shift_ref, scale_ref, g_ref)
    q_ref[...] = (_dot(h, w_ref[:, 0:d]) * Q_SCALE).astype(BF16)
    k_ref[...] = _dot(h, w_ref[:, d:2 * d])
    v_ref[...] = _dot(h, w_ref[:, 2 * d:3 * d])
    gate_ref[...] = _dot(h, w_ref[:, 3 * d:4 * d])


def _att_in_sample(x, shift, scale, g, w):
    r, d = x.shape
    sds = jax.ShapeDtypeStruct((r, d), F32)
    return pl.pallas_call(
        _att_in_sample_kernel,
        out_shape=(jax.ShapeDtypeStruct((r, d), BF16), sds, sds, sds),
        compiler_params=pltpu.CompilerParams(vmem_limit_bytes=VMEM_LIMIT),
        name="att_in_sample",
    )(x, shift, scale, g, w)


MASKED_LOG = -1e30


def _sb_scores(nqs, biases, k, masks):
    t = ATT_TILE
    nz_all = _dot(jnp.concatenate(nqs, axis=0), k)
    log_betas, splits, totals = [], [], []
    for c in range(len(nqs)):
        nz = nz_all[c * t:(c + 1) * t] - biases[c]
        s = jnp.log(1.0 + jnp.exp2(jnp.abs(nz) * (-LOG2_E)))
        lk = jnp.minimum(nz, 0.0) - s
        lb = lk - nz
        if masks[c] is not None:
            lk = jnp.where(masks[c], lk, 0.0)
            lb = jnp.where(masks[c], lb, MASKED_LOG)
        log_betas.append(lb)
        splits.append(jnp.concatenate(_split2(lk), axis=1))
        totals.append(jnp.sum(lk, axis=-1, keepdims=True))
    return log_betas, splits, totals


def _sb_apply(scores, v, u2, carries, accs):
    t = ATT_TILE
    log_betas, splits, totals = scores
    n = len(log_betas)
    later_all = _dot(jnp.concatenate(splits, axis=0), u2)
    weights = [jnp.exp(log_betas[c] + later_all[c * t:(c + 1) * t] + carries[c]).astype(BF16)
               for c in range(n)]
    av_all = _dot(jnp.concatenate(weights, axis=0), v)
    accs = [accs[c] + av_all[c * t:(c + 1) * t] for c in range(n)]
    carries = [carries[c] + totals[c] for c in range(n)]
    return carries, accs


def _sb_prompt_kernel(bias_ref, q_ref, kt_ref, v_ref, u2_ref, o_ref):
    t = ATT_TILE
    hp = pl.program_id(1)
    i = pl.program_id(2)
    lane = lax.broadcasted_iota(jnp.int32, (1, LANES), 1)
    row = lax.broadcasted_iota(jnp.int32, (t, t), 0)
    col = lax.broadcasted_iota(jnp.int32, (t, t), 1)
    causal = col < row
    nq = -q_ref[...].astype(F32)
    u2 = u2_ref[...]

    def values(j):
        return v_ref[pl.ds(pl.multiple_of(j * t, t), t), :]

    nqs, biases = [], []
    for hh in range(HEADS_PER_LANE_TILE):
        in_head = (lane >= hh * HEAD_DIM) & (lane < (hh + 1) * HEAD_DIM)
        masked = jnp.where(in_head, nq, 0.0).astype(BF16)
        nqs += [masked[0:t], masked[t:2 * t]]
        biases += [bias_ref[hp * HEADS_PER_LANE_TILE + hh]] * 2
    n = len(nqs)
    upper = [c for c in range(n) if c % 2 == 1]
    pick = lambda xs: [xs[c] for c in upper]

    carries = [jnp.zeros((t, 1), F32)] * n
    accs = [jnp.zeros((t, LANES), F32)] * n
    first = _sb_scores(pick(nqs), pick(biases), kt_ref[2 * i + 1], [causal] * len(upper))
    second = _sb_scores(nqs, biases, kt_ref[2 * i], [None if c in upper else causal for c in range(n)])
    cu, au = _sb_apply(first, values(2 * i + 1), u2, pick(carries), pick(accs))
    for idx, c in enumerate(upper):
        carries[c], accs[c] = cu[idx], au[idx]
    carries, accs = _sb_apply(second, values(2 * i), u2, carries, accs)

    def body(it, st):
        j = 2 * (i - it) - 1
        first = _sb_scores(nqs, biases, kt_ref[j], [None] * n)
        second = _sb_scores(nqs, biases, kt_ref[j - 1], [None] * n)
        cs, as_ = _sb_apply(first, values(j), u2, list(st[0]), list(st[1]))
        cs, as_ = _sb_apply(second, values(j - 1), u2, cs, as_)
        return tuple(cs), tuple(as_)

    carries, accs = lax.fori_loop(0, i, body, (tuple(carries), tuple(accs)))
    for r in range(2):
        o_ref[r * t:(r + 1) * t, :] = jnp.where(lane < HEAD_DIM, accs[r], accs[2 + r])


def _sb_prompt(bias, q, kbt, vb, u2, *, batch, seq):
    d = q.shape[1]
    t = ATT_TILE
    nq = seq // t
    steps = nq // 2
    return pl.pallas_call(
        _sb_prompt_kernel,
        out_shape=jax.ShapeDtypeStruct((batch * seq, d), F32),
        grid=(batch, d // LANES, steps),
        in_specs=[
            pl.BlockSpec(memory_space=pltpu.SMEM),
            pl.BlockSpec((2 * t, LANES), lambda b, h, i: (b * steps + i, h)),
            pl.BlockSpec((None, nq, LANES, t), lambda b, h, i: (b, 0, h, 0)),
            pl.BlockSpec((seq, LANES), lambda b, h, i: (b, h)),
            pl.BlockSpec((2 * t, t), lambda b, h, i: (0, 0)),
        ],
        out_specs=pl.BlockSpec((2 * t, LANES), lambda b, h, i: (b * steps + i, h)),
        compiler_params=_params(("arbitrary", "arbitrary", "arbitrary")),
        name="sb_prompt",
    )(bias, q, kbt, vb, u2)


def _sb_sample_kernel(pt_ref, q_ref, kn_ref, vn_ref, bias_ref, bias_row_ref, u_ref, ones_ref, hsel_ref,
                      *rest):
    g_n = PAGES_PER_STEP
    k_refs = rest[:g_n]
    v_refs = rest[g_n:2 * g_n]
    o_ref = rest[2 * g_n]
    qb_ref, acc_ref, carry_ref = rest[2 * g_n + 1:]
    c = pl.program_id(1)
    nc = pl.num_programs(1)
    d = D_MODEL
    ones = ones_ref[...]

    def new_token():
        mask_new = jnp.zeros((SUBLANES, LANES), jnp.int32) < jnp.zeros((SUBLANES, LANES), jnp.int32)
        qk = q_ref[...].astype(F32) * kn_ref[...].astype(BF16).astype(F32)
        z_new = _dot3(jnp.broadcast_to(qk, (SUBLANES, d)), hsel_ref[...]) + bias_row_ref[...]
        lk_new = _neg_softplus(z_new)
        a_new = jnp.where(mask_new, jnp.exp(z_new + lk_new), 0.0)
        return a_new, jnp.where(mask_new, lk_new, 0.0)

    @pl.when(c == 0)
    def _():
        blk = 256
        eye = (lax.broadcasted_iota(jnp.int32, (blk, blk), 0)
               == lax.broadcasted_iota(jnp.int32, (blk, blk), 1))
        for rb in range(d // blk):
            qrow = q_ref[:, rb * blk:(rb + 1) * blk].astype(F32)
            dq = jnp.where(eye, jnp.broadcast_to(qrow, (blk, blk)), 0.0).astype(BF16)
            qb_ref[rb * blk:(rb + 1) * blk, :] = _dot(dq, jnp.ones((blk, LANES), BF16))
        acc_ref[...] = jnp.zeros((d, PAGE), F32)
        _, lk_new = new_token()
        head_diag = (lax.broadcasted_iota(jnp.int32, (HEADS, LANES), 0)
                     == lax.broadcasted_iota(jnp.int32, (HEADS, LANES), 1))
        carry_ref[...] = _dot3(jnp.where(head_diag, jnp.broadcast_to(lk_new[0:1, :], (HEADS, LANES)), 0.0), ones)

    qb = qb_ref[...].reshape(HEADS, HEAD_DIM, PAGE)
    zs = [jnp.sum(k_refs[g][...].reshape(HEADS, HEAD_DIM, PAGE) * qb, axis=1) + bias_ref[...]
          for g in range(g_n)]
    z = jnp.concatenate(zs, axis=0)
    lk = _neg_softplus(z)
    hi, mid, lo = _split3(lk)
    u = u_ref[...]
    later = _dot(hi, u) + _dot(mid, u) + _dot(lo, u)
    tot = _dot(hi, ones) + _dot(mid, ones) + _dot(lo, ones)
    ex = z + lk + later
    off = carry_ref[...]
    a = [None] * g_n
    for g in reversed(range(g_n)):
        sl = slice(g * HEADS, (g + 1) * HEADS)
        a[g] = jnp.exp(ex[sl] + off)
        off = off + tot[sl]
    carry_ref[...] = off
    acc = acc_ref[...].reshape(HEADS, HEAD_DIM, PAGE)
    for g in range(g_n):
        acc = acc + a[g][:, None, :] * v_refs[g][...].reshape(HEADS, HEAD_DIM, PAGE)
    acc_ref[...] = acc.reshape(d, PAGE)

    @pl.when(c == nc - 1)
    def _():
        hi_a, mid_a, lo_a = _split3(acc.reshape(d, PAGE))
        ones8 = jnp.ones((SUBLANES, PAGE), BF16)
        o = _dot_nt(ones8, hi_a) + _dot_nt(ones8, mid_a) + _dot_nt(ones8, lo_a)
        a_new, _ = new_token()
        o_new = _dot_nt(a_new.astype(BF16), hsel_ref[...]) * vn_ref[...]
        o_ref[...] = o[0:1, :] + o_new[0:1, :]


def _sb_sample(page_table, q, k_new, v_new, bias, cache_kt, cache_vt):
    nb, d = q.shape
    n_pages = page_table.shape[1]
    g_n = PAGES_PER_STEP
    nc = n_pages // g_n
    i32 = jnp.int32

    key = jnp.arange(PAGE, dtype=i32)
    u = (key[:, None] > key[None, :]).astype(BF16)
    ones = jnp.ones((LANES, LANES), BF16)
    feat_head = jnp.arange(d, dtype=i32) // HEAD_DIM
    hsel = (feat_head[:, None] == jnp.arange(LANES, dtype=i32)[None, :]).astype(BF16)
    bias_col = jnp.broadcast_to(bias[:, None], (HEADS, PAGE))
    bias_row = jnp.pad(bias, (0, LANES - HEADS)).reshape(1, LANES)

    def page_spec(g):
        return pl.BlockSpec(
            (None, d, PAGE),
            lambda b, c, pt, g=g: (pt[b, (nc - 1 - c) * g_n + g], 0, 0))

    const2 = lambda b, c, pt: (0, 0)
    row3 = pl.BlockSpec((None, 1, d), lambda b, c, pt: (b, 0, 0))
    grid_spec = pltpu.PrefetchScalarGridSpec(
        num_scalar_prefetch=1,
        grid=(nb, nc),
        in_specs=[row3, row3, row3,
                  pl.BlockSpec((HEADS, PAGE), const2),
                  pl.BlockSpec((1, LANES), const2),
                  pl.BlockSpec((PAGE, PAGE), const2),
                  pl.BlockSpec((LANES, LANES), const2),
                  pl.BlockSpec((d, LANES), const2)]
                 + [page_spec(g) for g in range(g_n)] * 2,
        out_specs=row3,
        scratch_shapes=[pltpu.VMEM((d, LANES), F32),
                        pltpu.VMEM((d, PAGE), F32),
                        pltpu.VMEM((HEADS, PAGE), F32)],
    )
    out = pl.pallas_call(
        _sb_sample_kernel,
        out_shape=jax.ShapeDtypeStruct((nb, 1, d), F32),
        grid_spec=grid_spec,
        compiler_params=_params(("arbitrary", "arbitrary")),
        name="sb_sample",
    )(page_table, q.reshape(nb, 1, d), k_new.reshape(nb, 1, d), v_new.reshape(nb, 1, d),
      bias_col, bias_row, u, ones, hsel, *([cache_kt] * g_n), *([cache_vt] * g_n))
    return out.reshape(nb, d)


def _att_out_rg_in_kernel(o_ref, gate_ref, x_ref, mgate_ref, shift_ref, scale_ref, g_ref,
                          wout_ref, win_ref, x1_ref, xbr_ref, gbr_ref):
    y = _dot((o_ref[...] * _silu(gate_ref[...])).astype(BF16), wout_ref[...])
    x1 = x_ref[...] + mgate_ref[...] * y
    x1_ref[...] = x1
    h = (_rmsnorm(x1, g_ref[...]) * (1.0 + scale_ref[...]) + shift_ref[...]).astype(BF16)
    d = D_MODEL
    xbr_ref[...] = _dot(h, win_ref[:, 0:d])
    gbr_ref[...] = _dot(h, win_ref[:, d:2 * d])


def _att_out_rg_in(o, gate, x, mgate, shift, scale, g, w_out, w_in, *, tm, rows_per_mod):
    r, d = x.shape
    mod_rows = shift.shape[1]
    row = pl.BlockSpec((tm, d), lambda i: (i, 0))
    mod = pl.BlockSpec((None, mod_rows, d), lambda i: ((i * tm) // rows_per_mod, 0, 0))
    sds = jax.ShapeDtypeStruct((r, d), F32)
    return pl.pallas_call(
        _att_out_rg_in_kernel,
        out_shape=(sds, sds, sds),
        grid=(r // tm,),
        in_specs=[row, row, row, mod, mod, mod,
                  _resident((1, d)), _resident((d, d)), _resident((d, 2 * d))],
        out_specs=(row, row, row),
        compiler_params=_params(("arbitrary",)),
        name="att_out_rg_in",
    )(o, gate, x, mgate, shift, scale, g, w_out, w_in)


def _rg_gates(xc, wa_ref, ba_ref, wx_ref, bx_ref, lam_ref):
    xcb = xc.astype(BF16)
    w = RG_BLOCK_W
    ra = jnp.concatenate([_dot(xcb[:, n * w:(n + 1) * w], wa_ref[n]) for n in range(RG_BLOCKS)], axis=-1)
    rx = jnp.concatenate([_dot(xcb[:, n * w:(n + 1) * w], wx_ref[n]) for n in range(RG_BLOCKS)], axis=-1)
    r = _sigmoid(ra + ba_ref[...])
    ig = _sigmoid(rx + bx_ref[...])
    lam = lam_ref[...]
    log_sig_lam = -(jnp.maximum(-lam, 0.0) + jnp.log1p(jnp.exp(-jnp.abs(lam))))
    log_a = RG_C * r * log_sig_lam
    a = jnp.exp(log_a)
    gap = -jnp.tanh(log_a) * (a * a + 1.0)
    u = (gap * lax.rsqrt(jnp.maximum(gap, F32_TINY))) * (ig * xc)
    return a, u


def _rg_finish(y, gbr, x1, mgate, wout_ref, fg_ref):
    out = _dot((y * _silu(gbr)).astype(BF16), wout_ref[...])
    return _rmsnorm(x1 + mgate * out, fg_ref[...])


def _rg_prompt_kernel(xbr_ref, gbr_ref, x1_ref, mgate_ref, cw_ref, cb_ref, wa_ref, ba_ref,
                      wx_ref, bx_ref, lam_ref, wout_ref, fg_ref,
                      y_ref, tail_ref, hlast_ref, xx_s, a_s, u_s, h_s):
    tt = RG_TILE
    d = D_MODEL
    ti = pl.program_id(1)
    pad = SUBLANES

    @pl.when(ti == 0)
    def _():
        xx_s[0:pad, :] = jnp.zeros((pad, d), F32)
        h_s[...] = jnp.zeros((SUBLANES, d), F32)

    x = xbr_ref[...]
    xx_s[pad:pad + tt, :] = x
    xc = cb_ref[...] + cw_ref[0:1, :] * xx_s[pad - 3:pad - 3 + tt, :]
    xc = xc + cw_ref[1:2, :] * xx_s[pad - 2:pad - 2 + tt, :]
    xc = xc + cw_ref[2:3, :] * xx_s[pad - 1:pad - 1 + tt, :]
    xc = xc + cw_ref[3:4, :] * x
    xx_s[0:pad, :] = x[tt - pad:tt, :]

    a, u = _rg_gates(xc, wa_ref, ba_ref, wx_ref, bx_ref, lam_ref)
    a_s[...] = a
    u_s[...] = u
    sub = lax.broadcasted_iota(jnp.int32, (SUBLANES, d), 0)

    def group(i, h):
        r0 = pl.multiple_of(i * SUBLANES, SUBLANES)
        ag = a_s[pl.ds(r0, SUBLANES), :]
        ug = u_s[pl.ds(r0, SUBLANES), :]
        for s in (1, 2, 4):
            a_prev = jnp.where(sub >= s, pltpu.roll(ag, s, axis=0), 1.0)
            u_prev = jnp.where(sub >= s, pltpu.roll(ug, s, axis=0), 0.0)
            ug = ag * u_prev + ug
            ag = ag * a_prev
        hg = ag * h + ug
        u_s[pl.ds(r0, SUBLANES), :] = hg
        return jnp.broadcast_to(hg[SUBLANES - 1:SUBLANES, :], (SUBLANES, d))

    h = lax.fori_loop(0, tt // SUBLANES, group, h_s[...], unroll=4)
    h_s[...] = h

    y_ref[...] = _rg_finish(u_s[...], gbr_ref[...], x1_ref[...], mgate_ref[...], wout_ref, fg_ref)

    @pl.when(ti == pl.num_programs(1) - 1)
    def _():
        tail_ref[...] = x[tt - (CONV_W - 1):tt, :]
        hlast_ref[...] = h[0:1, :]


def _rg_prompt(xbr, gbr, x1, mgate, cw, cb, wa, ba, wx, bx, lam, wout, fg, *, batch, seq):
    d = D_MODEL
    tt = RG_TILE
    nt = seq // tt
    row = pl.BlockSpec((tt, d), lambda b, t: (b * nt + t, 0))
    vec = _resident((1, d))
    gates_w = _resident((RG_BLOCKS, RG_BLOCK_W, RG_BLOCK_W))
    return pl.pallas_call(
        _rg_prompt_kernel,
        out_shape=(jax.ShapeDtypeStruct((batch * seq, d), F32),
                   jax.ShapeDtypeStruct((batch, CONV_W - 1, d), F32),
                   jax.ShapeDtypeStruct((batch, 1, d), F32)),
        grid=(batch, nt),
        in_specs=[row, row, row,
                  pl.BlockSpec((None, 1, d), lambda b, t: (b, 0, 0)),
                  _resident((CONV_W, d)),
                  vec, gates_w, vec, gates_w, vec, vec,
                  _resident((d, d)),
                  vec],
        out_specs=(row,
                   pl.BlockSpec((None, CONV_W - 1, d), lambda b, t: (b, 0, 0)),
                   pl.BlockSpec((None, 1, d), lambda b, t: (b, 0, 0))),
        scratch_shapes=[pltpu.VMEM((tt + SUBLANES, d), F32),
                        pltpu.VMEM((tt, d), F32),
                        pltpu.VMEM((tt, d), F32),
                        pltpu.VMEM((SUBLANES, d), F32)],
        compiler_params=_params(("arbitrary", "arbitrary")),
        name="rg_prompt",
    )(xbr, gbr, x1, mgate, cw, cb, wa, ba, wx, bx, lam, wout, fg)


def _rg_sample_kernel(xbr_ref, gbr_ref, x1_ref, mgate_ref, sconv_ref, sh_ref, cw_ref, cb_ref,
                      wa_ref, ba_ref, wx_ref, bx_ref, lam_ref, wout_ref, fg_ref,
                      y_ref, tail_ref, h_ref):
    x = xbr_ref[...]
    xc = cb_ref[...] + cw_ref[0:1, :] * sconv_ref[0]
    xc = xc + cw_ref[1:2, :] * sconv_ref[1]
    xc = xc + cw_ref[2:3, :] * sconv_ref[2]
    xc = xc + cw_ref[3:4, :] * x
    a, u = _rg_gates(xc, wa_ref, ba_ref, wx_ref, bx_ref, lam_ref)
    h = a * sh_ref[...] + u
    h_ref[...] = h
    tail_ref[0] = sconv_ref[1]
    tail_ref[1] = sconv_ref[2]
    tail_ref[2] = x
    y_ref[...] = _rg_finish(h, gbr_ref[...], x1_ref[...], mgate_ref[...], wout_ref, fg_ref)


def _rg_sample(xbr, gbr, x1, mgate, sconv, sh, cw, cb, wa, ba, wx, bx, lam, wout, fg):
    nb, d = xbr.shape
    sds = jax.ShapeDtypeStruct((nb, d), F32)
    return pl.pallas_call(
        _rg_sample_kernel,
        out_shape=(sds, jax.ShapeDtypeStruct((CONV_W - 1, nb, d), F32), sds),
        compiler_params=pltpu.CompilerParams(vmem_limit_bytes=VMEM_LIMIT),
        name="rg_sample",
    )(xbr, gbr, x1, mgate, sconv, sh, cw, cb, wa, ba, wx, bx, lam, wout, fg)


def kernel(x_prompt, x_sample, c_prompt, c_sample, cache_k, cache_v, page_table, state_conv, state_h,
           norm_g, mod_w, mod_b, att_w_in, att_bias, att_w_out, rg_w_in, rg_conv_w, rg_conv_b,
           rg_w_a, rg_b_a, rg_w_x, rg_b_x, rg_lambda, rg_w_out, final_g):
    batch, seq, d = x_prompt.shape
    nb, dec_seq, _ = x_sample.shape
    assert d == D_MODEL and dec_seq == 1 and mod_w.shape[0] == 2

    mod = _modulation(jnp.concatenate([c_prompt, c_sample], axis=0), mod_w, mod_b)

    def mod_parts(layer, lo, hi, shape):
        m = mod[layer, lo:hi]
        return tuple(m[:, i * d:(i + 1) * d].reshape(shape) for i in range(3))

    shift0p, scale0p, gate0p = mod_parts(0, 0, batch, (batch, 1, d))
    shift1p, scale1p, gate1p = mod_parts(1, 0, batch, (batch, 1, d))
    shift0s, scale0s, gate0s = mod_parts(0, batch, batch + nb, (nb, d))
    shift1s, scale1s, gate1s = mod_parts(1, batch, batch + nb, (nb, d))

    g0, g1 = norm_g[0].reshape(1, d), norm_g[1].reshape(1, d)
    w_att_in = att_w_in[0].astype(BF16)
    w_kv_t = jnp.swapaxes(att_w_in[0][:, d:3 * d], 0, 1).astype(BF16)
    w_att_out = att_w_out[0].astype(BF16)
    w_rg_in = rg_w_in[0].astype(BF16)
    w_rg_out = rg_w_out[0].astype(BF16)
    wa, wx = rg_w_a[0].astype(BF16), rg_w_x[0].astype(BF16)
    vec = lambda v: v.reshape(1, d)
    rg_consts = (rg_conv_w[0], vec(rg_conv_b[0]), wa, vec(rg_b_a[0]), wx, vec(rg_b_x[0]),
                 vec(rg_lambda[0]), w_rg_out, vec(final_g))
    bias = att_bias[0]

    xp = x_prompt.reshape(batch * seq, d)
    q_p, kt_p, vt_p, gate_p, kbt_p, vb_p = _att_in_prompt(xp, shift0p, scale0p, g0, w_att_in, w_kv_t,
                                                          batch=batch, seq=seq)
    idx = jnp.arange(ATT_TILE, dtype=jnp.int32)
    u = (idx[:, None] > idx[None, :]).astype(BF16)
    o_p = _sb_prompt(bias, q_p, kbt_p, vb_p, jnp.concatenate([u, u], axis=0), batch=batch, seq=seq)
    x1_p, xbr_p, gbr_p = _att_out_rg_in(o_p, gate_p, xp, gate0p, shift1p, scale1p, g1, w_att_out, w_rg_in,
                                        tm=ROW_TILE, rows_per_mod=seq)
    y_p, tail_p, hlast_p = _rg_prompt(xbr_p, gbr_p, x1_p, gate1p, *rg_consts, batch=batch, seq=seq)

    xs = x_sample.reshape(nb, d)
    q_s, k_s, v_s, gate_s = _att_in_sample(xs, shift0s, scale0s, g0, w_att_in)
    pages_t = lambda cache: jnp.transpose(cache, (0, 1, 3, 4, 2)).reshape(-1, d, PAGE)
    o_s = _sb_sample(page_table, q_s, k_s, v_s, bias, pages_t(cache_k), pages_t(cache_v))
    as3 = lambda m: m.reshape(1, nb, d)
    x1_s, xbr_s, gbr_s = _att_out_rg_in(o_s, gate_s, xs, as3(gate0s), as3(shift1s), as3(scale1s), g1,
                                        w_att_out, w_rg_in, tm=nb, rows_per_mod=nb)
    y_s, tail_s, h_s = _rg_sample(xbr_s, gbr_s, x1_s, gate1s,
                                  jnp.swapaxes(state_conv[0], 0, 1), state_h[0], *rg_consts)

    def kv_rows(t):
        return jnp.transpose(t.reshape(1, batch, HEADS, HEAD_DIM, seq), (0, 1, 4, 2, 3))

    kv_s = (1, nb, 1, HEADS, HEAD_DIM)
    return (y_p.reshape(batch, seq, d), y_s.reshape(nb, 1, d),
            kv_rows(kt_p), kv_rows(vt_p), k_s.reshape(kv_s), v_s.reshape(kv_s),
            tail_p.reshape(1, batch, CONV_W - 1, d), hlast_p.reshape(1, batch, d),
            jnp.swapaxes(tail_s, 0, 1).reshape(1, nb, CONV_W - 1, d), h_s.reshape(1, nb, d))
```

```python
import functools

import jax
import jax.numpy as jnp
from jax import lax
from jax.experimental import pallas as pl
from jax.experimental.pallas import tpu as pltpu

F32 = jnp.float32
BF16 = jnp.bfloat16

D_MODEL = 1024
HEADS = 16
HEAD_DIM = 64
LANES = 128
SUBLANES = 8
HEADS_PER_LANE_TILE = LANES // HEAD_DIM
PAGE = 128
RG_BLOCKS = 4
RG_BLOCK_W = D_MODEL // RG_BLOCKS
CONV_W = 4
RG_C = 8.0
EPS = 1e-6
Q_SCALE = HEAD_DIM ** -0.5
LOG2_E = 1.4426950408889634
F32_TINY = 1.1754943508222875e-38

ROW_TILE = 512
ATT_TILE = 256
RG_TILE = 256
PAGES_PER_STEP = 8
CHUNKS_PER_STEP = 2
VMEM_LIMIT = 56 * 1024 * 1024

NT_DIMS = (((1,), (1,)), ((), ()))


def _dot(a, b):
    return jnp.dot(a, b, preferred_element_type=F32)


def _dot_nt(a, b):
    return lax.dot_general(a, b, NT_DIMS, preferred_element_type=F32)


def _sigmoid(x):
    return 0.5 * jnp.tanh(0.5 * x) + 0.5


def _silu(x):
    return x * _sigmoid(x)


def _neg_softplus(z):
    return -(jnp.maximum(z, 0.0) + jnp.log(1.0 + jnp.exp(-jnp.abs(z))))


def _split2(x):
    hi = x.astype(BF16)
    lo = (x - hi.astype(F32)).astype(BF16)
    return hi, lo


def _split3(x):
    hi = x.astype(BF16)
    r = x - hi.astype(F32)
    mid = r.astype(BF16)
    lo = (r - mid.astype(F32)).astype(BF16)
    return hi, mid, lo


def _dot3(x, m):
    hi, mid, lo = _split3(x)
    return _dot(hi, m) + _dot(mid, m) + _dot(lo, m)


def _rmsnorm(x, g):
    ms = jnp.mean(x * x, axis=-1, keepdims=True)
    return (x * lax.rsqrt(ms + EPS)) * g


def _params(semantics):
    return pltpu.CompilerParams(dimension_semantics=semantics, vmem_limit_bytes=VMEM_LIMIT)


def _resident(shape):
    zeros = (0,) * len(shape)
    return pl.BlockSpec(shape, lambda *_: zeros, pipeline_mode=pl.Buffered(1))


def _mod_kernel(c_ref, w_ref, b_ref, o_ref):
    c = c_ref[...]
    o_ref[...] = _dot(_silu(c).astype(BF16), w_ref[...].astype(BF16)) + b_ref[...]


def _modulation(c_all, mod_w, mod_b):
    depth, d, n = mod_w.shape
    rows = c_all.shape[0]
    tn = 1024
    return pl.pallas_call(
        _mod_kernel,
        out_shape=jax.ShapeDtypeStruct((depth, rows, n), F32),
        grid=(depth, n // tn),
        in_specs=[
            pl.BlockSpec((rows, d), lambda l, j: (0, 0)),
            pl.BlockSpec((None, d, tn), lambda l, j: (l, 0, j)),
            pl.BlockSpec((None, 1, tn), lambda l, j: (l, 0, j)),
        ],
        out_specs=pl.BlockSpec((None, rows, tn), lambda l, j: (l, 0, j)),
        compiler_params=_params(("arbitrary", "arbitrary")),
        name="modulation",
    )(c_all, mod_w, mod_b.reshape(depth, 1, n))


def _att_in_hidden(x_ref, shift_ref, scale_ref, g_ref):
    h = _rmsnorm(x_ref[...], g_ref[...]) * (1.0 + scale_ref[...]) + shift_ref[...]
    return h.astype(BF16)


def _att_in_prompt_kernel(x_ref, shift_ref, scale_ref, g_ref, w_ref, wkv_t_ref,
                          q_ref, kt_ref, vt_ref, gate_ref, kbt_ref, vb_ref):
    d = D_MODEL
    t = ATT_TILE
    h = _att_in_hidden(x_ref, shift_ref, scale_ref, g_ref)
    q_ref[...] = (_dot(h, w_ref[:, 0:d]) * Q_SCALE).astype(BF16)
    kt = _dot_nt(wkv_t_ref[0:d, :], h)
    kt_ref[...] = kt
    for j in range(kbt_ref.shape[0]):
        kbt_ref[j] = kt[:, j * t:(j + 1) * t].astype(BF16)
    vt_ref[...] = _dot_nt(wkv_t_ref[d:2 * d, :], h)
    vb_ref[...] = _dot(h, w_ref[:, 2 * d:3 * d]).astype(BF16)
    gate_ref[...] = _dot(h, w_ref[:, 3 * d:4 * d])


def _att_in_prompt(x, shift, scale, g, w, wkv_t, *, batch, seq):
    r, d = x.shape
    tm = ROW_TILE
    t = ATT_TILE
    per_b = seq // tm
    row = pl.BlockSpec((tm, d), lambda i: (i, 0))
    mod = pl.BlockSpec((None, 1, d), lambda i: (i // per_b, 0, 0))
    feat = pl.BlockSpec((None, d, tm), lambda i: (i // per_b, 0, i % per_b))
    feat_tiles = pl.BlockSpec((None, tm // t, d, t), lambda i: (i // per_b, i % per_b, 0, 0))
    return pl.pallas_call(
        _att_in_prompt_kernel,
        out_shape=(
            jax.ShapeDtypeStruct((r, d), BF16),
            jax.ShapeDtypeStruct((batch, d, seq), F32),
            jax.ShapeDtypeStruct((batch, d, seq), F32),
            jax.ShapeDtypeStruct((r, d), F32),
            jax.ShapeDtypeStruct((batch, seq // t, d, t), BF16),
            jax.ShapeDtypeStruct((r, d), BF16),
        ),
        grid=(r // tm,),
        in_specs=[row, mod, mod, _resident((1, d)), _resident((d, 4 * d)), _resident((2 * d, d))],
        out_specs=(row, feat, feat, row, feat_tiles, row),
        compiler_params=_params(("arbitrary",)),
        name="att_in_prompt",
    )(x, shift, scale, g, w, wkv_t)


def _att_in_sample_kernel(x_ref, shift_ref, scale_ref, g_ref, w_ref, q_ref, k_ref, v_ref, gate_ref):
    d = D_MODEL
    h = _att_in_hidden(x_ref, shift_ref, scale_ref, g_ref)
    q_ref[...] = (_dot(h, w_ref[:, 0:d]) * Q_SCALE).astype(BF16)
    k_ref[...] = _dot(h, w_ref[:, d:2 * d])
    v_ref[...] = _dot(h, w_ref[:, 2 * d:3 * d])
    gate_ref[...] = _dot(h, w_ref[:, 3 * d:4 * d])


def _att_in_sample(x, shift, scale, g, w):
    r, d = x.shape
    sds = jax.ShapeDtypeStruct((r, d), F32)
    return pl.pallas_call(
        _att_in_sample_kernel,
        out_shape=(jax.ShapeDtypeStruct((r, d), BF16), sds, sds, sds),
        compiler_params=pltpu.CompilerParams(vmem_limit_bytes=VMEM_LIMIT),
        name="att_in_sample",
    )(x, shift, scale, g, w)


MASKED_LOG = -1e30


def _sb_scores(nqs, biases, k, masks):
    t = ATT_TILE
    nz_all = _dot(jnp.concatenate(nqs, axis=0), k)
    log_betas, splits, totals = [], [], []
    for c in range(len(nqs)):
        nz = nz_all[c * t:(c + 1) * t] - biases[c]
        s = jnp.log(1.0 + jnp.exp2(jnp.abs(nz) * (-LOG2_E)))
        lk = jnp.minimum(nz, 0.0) - s
        lb = lk - nz
        if masks[c] is not None:
            lk = jnp.where(masks[c], lk, 0.0)
            lb = jnp.where(masks[c], lb, MASKED_LOG)
        log_betas.append(lb)
        splits.append(jnp.concatenate(_split2(lk), axis=1))
        totals.append(jnp.sum(lk, axis=-1, keepdims=True))
    return log_betas, splits, totals


def _sb_apply(scores, v, u2, carries, accs):
    t = ATT_TILE
    log_betas, splits, totals = scores
    n = len(log_betas)
    later_all = _dot(jnp.concatenate(splits, axis=0), u2)
    weights = [jnp.exp(log_betas[c] + later_all[c * t:(c + 1) * t] + carries[c]).astype(BF16)
               for c in range(n)]
    av_all = _dot(jnp.concatenate(weights, axis=0), v)
    accs = [accs[c] + av_all[c * t:(c + 1) * t] for c in range(n)]
    carries = [carries[c] + totals[c] for c in range(n)]
    return carries, accs


def _sb_prompt_step(bias_ref, q_ref, kt_ref, v_ref, u2_ref, o_ref, hp, i):
    t = ATT_TILE
    lane = lax.broadcasted_iota(jnp.int32, (1, LANES), 1)
    row = lax.broadcasted_iota(jnp.int32, (t, t), 0)
    col = lax.broadcasted_iota(jnp.int32, (t, t), 1)
    causal = col < row
    nq = -q_ref[...].astype(F32)
    u2 = u2_ref[...]

    def values(j):
        return v_ref[pl.ds(pl.multiple_of(j * t, t), t), :]

    nqs, biases = [], []
    for hh in range(HEADS_PER_LANE_TILE):
        in_head = (lane >= hh * HEAD_DIM) & (lane < (hh + 1) * HEAD_DIM)
        masked = jnp.where(in_head, nq, 0.0).astype(BF16)
        nqs += [masked[0:t], masked[t:2 * t]]
        biases += [bias_ref[hp * HEADS_PER_LANE_TILE + hh]] * 2
    n = len(nqs)
    upper = [c for c in range(n) if c % 2 == 1]
    pick = lambda xs: [xs[c] for c in upper]

    carries = [jnp.zeros((t, 1), F32)] * n
    accs = [jnp.zeros((t, LANES), F32)] * n
    first = _sb_scores(pick(nqs), pick(biases), kt_ref[2 * i + 1], [causal] * len(upper))
    second = _sb_scores(nqs, biases, kt_ref[2 * i], [None if c in upper else causal for c in range(n)])
    cu, au = _sb_apply(first, values(2 * i + 1), u2, pick(carries), pick(accs))
    for idx, c in enumerate(upper):
        carries[c], accs[c] = cu[idx], au[idx]
    carries, accs = _sb_apply(second, values(2 * i), u2, carries, accs)

    def body(it, st):
        j = 2 * (i - it) - 1
        first = _sb_scores(nqs, biases, kt_ref[j], [None] * n)
        second = _sb_scores(nqs, biases, kt_ref[j - 1], [None] * n)
        cs, as_ = _sb_apply(first, values(j), u2, list(st[0]), list(st[1]))
        cs, as_ = _sb_apply(second, values(j - 1), u2, cs, as_)
        return tuple(cs), tuple(as_)

    carries, accs = lax.fori_loop(0, i, body, (tuple(carries), tuple(accs)))
    for r in range(2):
        o_ref[r * t:(r + 1) * t, :] = jnp.where(lane < HEAD_DIM, accs[r], accs[2 + r])


def _sb_sample_chunk(k_refs, v_refs, bias_ref, u_ref, ones, qb_ref, acc_ref, carry_ref, z_ref, a_ref):
    g_n = len(k_refs)
    hd = HEAD_DIM
    for h in range(HEADS):
        feat = slice(h * hd, (h + 1) * hd)
        qb_h = qb_ref[feat, :]
        for g in range(g_n):
            part = (k_refs[g][feat, :] * qb_h).reshape(hd // SUBLANES, SUBLANES, PAGE).sum(axis=0)
            z_ref[g * HEADS + h:g * HEADS + h + 1, :] = jnp.sum(part, axis=0, keepdims=True)
    z = z_ref[...] + bias_ref[...]
    lk = _neg_softplus(z)
    hi, mid, lo = _split3(lk)
    u = u_ref[...]
    later = _dot(hi, u) + _dot(mid, u) + _dot(lo, u)
    tot = _dot(hi, ones) + _dot(mid, ones) + _dot(lo, ones)
    ex = z + lk + later
    off = carry_ref[...]
    for g in reversed(range(g_n)):
        sl = slice(g * HEADS, (g + 1) * HEADS)
        a_ref[sl, :] = jnp.exp(ex[sl] + off)
        off = off + tot[sl]
    carry_ref[...] = off
    for h in range(HEADS):
        feat = slice(h * hd, (h + 1) * hd)
        acc = acc_ref[feat, :]
        for g in range(g_n):
            acc = acc + a_ref[g * HEADS + h:g * HEADS + h + 1, :] * v_refs[g][feat, :]
        acc_ref[feat, :] = acc


def _sb_sample_step(first, last, q_ref, kn_ref, vn_ref, bias_ref, bias_row_ref, u_ref, ones_ref, hsel_ref,
                    k_chunks, v_chunks, o_ref, qb_ref, acc_ref, carry_ref, z_ref, a_ref, *, beside):
    d = D_MODEL
    ones = ones_ref[...]

    def new_token():
        mask_new = jnp.zeros((SUBLANES, LANES), jnp.int32) < jnp.zeros((SUBLANES, LANES), jnp.int32)
        qk = q_ref[...].astype(F32) * kn_ref[...].astype(BF16).astype(F32)
        z_new = _dot3(jnp.broadcast_to(qk, (SUBLANES, d)), hsel_ref[...]) + bias_row_ref[...]
        lk_new = _neg_softplus(z_new)
        a_new = jnp.where(mask_new, jnp.exp(z_new + lk_new), 0.0)
        return a_new, jnp.where(mask_new, lk_new, 0.0)

    @pl.when(first)
    def _():
        blk = 256
        eye = (lax.broadcasted_iota(jnp.int32, (blk, blk), 0)
               == lax.broadcasted_iota(jnp.int32, (blk, blk), 1))
        for rb in range(d // blk):
            qrow = q_ref[:, rb * blk:(rb + 1) * blk].astype(F32)
            dq = jnp.where(eye, jnp.broadcast_to(qrow, (blk, blk)), 0.0).astype(BF16)
            qb_ref[rb * blk:(rb + 1) * blk, :] = _dot(dq, jnp.ones((blk, LANES), BF16))
        acc_ref[...] = jnp.zeros((d, PAGE), F32)
        _, lk_new = new_token()
        head_diag = (lax.broadcasted_iota(jnp.int32, (HEADS, LANES), 0)
                     == lax.broadcasted_iota(jnp.int32, (HEADS, LANES), 1))
        carry_ref[...] = _dot3(jnp.where(head_diag, jnp.broadcast_to(lk_new[0:1, :], (HEADS, LANES)), 0.0), ones)

    for k_refs, v_refs in zip(k_chunks, v_chunks):
        _sb_sample_chunk(k_refs, v_refs, bias_ref, u_ref, ones, qb_ref, acc_ref, carry_ref, z_ref, a_ref)
    beside()

    @pl.when(last)
    def _():
        hi_a, mid_a, lo_a = _split3(acc_ref[...])
        ones8 = jnp.ones((SUBLANES, PAGE), BF16)
        o = _dot_nt(ones8, hi_a) + _dot_nt(ones8, mid_a) + _dot_nt(ones8, lo_a)
        a_new, _ = new_token()
        o_new = _dot_nt(a_new.astype(BF16), hsel_ref[...]) * vn_ref[...]
        o_ref[...] = o[0:1, :] + o_new[0:1, :]


def _sb_attention_kernel(pt_ref, bias_ref, q_ref, kt_ref, v_ref, u2_ref,
                         qs_ref, kn_ref, vn_ref, bias_col_ref, bias_row_ref, u_ref, ones_ref, hsel_ref, *rest):
    g_n, n_chunks = PAGES_PER_STEP, CHUNKS_PER_STEP
    n = g_n * n_chunks
    chunks = lambda refs: [refs[c * g_n:(c + 1) * g_n] for c in range(n_chunks)]
    k_chunks, v_chunks = chunks(rest[:n]), chunks(rest[n:2 * n])
    o_ref, os_ref = rest[2 * n:2 * n + 2]
    scratch = rest[2 * n + 2:]
    b, hp, i = pl.program_id(0), pl.program_id(1), pl.program_id(2)
    step = (b * pl.num_programs(1) + hp) * pl.num_programs(2) + i
    per_seq = (pt_ref.shape[1] // g_n) // n_chunks
    part = step % per_seq
    prompt = functools.partial(_sb_prompt_step, bias_ref, q_ref, kt_ref, v_ref, u2_ref, o_ref, hp, i)
    _sb_sample_step(part == 0, part == per_seq - 1, qs_ref, kn_ref, vn_ref, bias_col_ref, bias_row_ref,
                    u_ref, ones_ref, hsel_ref, k_chunks, v_chunks, os_ref, *scratch, beside=prompt)


def _sb_attention(page_table, bias, q, kbt, vb, q_s, k_new, v_new, cache_kt, cache_vt, *, batch, seq):
    d = q.shape[1]
    t = ATT_TILE
    nq = seq // t
    steps = nq // 2
    nb = q_s.shape[0]
    g_n, n_chunks = PAGES_PER_STEP, CHUNKS_PER_STEP
    nc = page_table.shape[1] // g_n
    per_seq = nc // n_chunks
    heads2 = d // LANES
    assert batch * heads2 * steps == nb * per_seq
    i32 = jnp.int32

    idx = jnp.arange(t, dtype=i32)
    u_t = (idx[:, None] > idx[None, :]).astype(BF16)
    u2 = jnp.concatenate([u_t, u_t], axis=0)
    key = jnp.arange(PAGE, dtype=i32)
    u = (key[:, None] > key[None, :]).astype(BF16)
    ones = jnp.ones((LANES, LANES), BF16)
    feat_head = jnp.arange(d, dtype=i32) // HEAD_DIM
    hsel = (feat_head[:, None] == jnp.arange(LANES, dtype=i32)[None, :]).astype(BF16)
    bias_col = jnp.broadcast_to(jnp.tile(bias, g_n)[:, None], (g_n * HEADS, PAGE))
    bias_row = jnp.pad(bias, (0, LANES - HEADS)).reshape(1, LANES)

    def step_of(b, h, i):
        return (b * heads2 + h) * steps + i

    def page_spec(c, g):
        def index(b, h, i, pt, _):
            s = step_of(b, h, i)
            chunk = (s % per_seq) * n_chunks + c
            return (pt[s // per_seq, (nc - 1 - chunk) * g_n + g], 0, 0)
        return pl.BlockSpec((None, d, PAGE), index)

    const2 = lambda b, h, i, pt, _: (0, 0)
    row3 = pl.BlockSpec((None, 1, d), lambda b, h, i, pt, _: (step_of(b, h, i) // per_seq, 0, 0))
    pages = [page_spec(c, g) for c in range(n_chunks) for g in range(g_n)]
    grid_spec = pltpu.PrefetchScalarGridSpec(
        num_scalar_prefetch=2,
        grid=(batch, heads2, steps),
        in_specs=[pl.BlockSpec((2 * t, LANES), lambda b, h, i, pt, _: (b * steps + i, h)),
                  pl.BlockSpec((None, nq, LANES, t), lambda b, h, i, pt, _: (b, 0, h, 0)),
                  pl.BlockSpec((seq, LANES), lambda b, h, i, pt, _: (b, h)),
                  pl.BlockSpec((2 * t, t), const2),
                  row3, row3, row3,
                  pl.BlockSpec((g_n * HEADS, PAGE), const2),
                  pl.BlockSpec((1, LANES), const2),
                  pl.BlockSpec((PAGE, PAGE), const2),
                  pl.BlockSpec((LANES, LANES), const2),
                  pl.BlockSpec((d, LANES), const2)]
                 + pages * 2,
        out_specs=(pl.BlockSpec((2 * t, LANES), lambda b, h, i, pt, _: (b * steps + i, h)), row3),
        scratch_shapes=[pltpu.VMEM((d, LANES), F32),
                        pltpu.VMEM((d, PAGE), F32),
                        pltpu.VMEM((HEADS, PAGE), F32),
                        pltpu.VMEM((g_n * HEADS, PAGE), F32),
                        pltpu.VMEM((g_n * HEADS, PAGE), F32)],
    )
    n = g_n * n_chunks
    o_p, o_s = pl.pallas_call(
        _sb_attention_kernel,
        out_shape=(jax.ShapeDtypeStruct((batch * seq, d), F32), jax.ShapeDtypeStruct((nb, 1, d), F32)),
        grid_spec=grid_spec,
        compiler_params=_params(("arbitrary", "arbitrary", "arbitrary")),
        name="sb_attention",
    )(page_table, bias, q, kbt, vb, u2,
      q_s.reshape(nb, 1, d), k_new.reshape(nb, 1, d), v_new.reshape(nb, 1, d),
      bias_col, bias_row, u, ones, hsel, *([cache_kt] * n), *([cache_vt] * n))
    return o_p, o_s.reshape(nb, d)


def _att_out_rg_in_kernel(o_ref, gate_ref, x_ref, mgate_ref, shift_ref, scale_ref, g_ref,
                          wout_ref, win_ref, x1_ref, xbr_ref, gbr_ref):
    y = _dot((o_ref[...] * _silu(gate_ref[...])).astype(BF16), wout_ref[...])
    x1 = x_ref[...] + mgate_ref[...] * y
    x1_ref[...] = x1
    h = (_rmsnorm(x1, g_ref[...]) * (1.0 + scale_ref[...]) + shift_ref[...]).astype(BF16)
    d = D_MODEL
    xbr_ref[...] = _dot(h, win_ref[:, 0:d])
    gbr_ref[...] = _dot(h, win_ref[:, d:2 * d])


def _att_out_rg_in(o, gate, x, mgate, shift, scale, g, w_out, w_in, *, tm, rows_per_mod):
    r, d = x.shape
    mod_rows = shift.shape[1]
    row = pl.BlockSpec((tm, d), lambda i: (i, 0))
    mod = pl.BlockSpec((None, mod_rows, d), lambda i: ((i * tm) // rows_per_mod, 0, 0))
    sds = jax.ShapeDtypeStruct((r, d), F32)
    return pl.pallas_call(
        _att_out_rg_in_kernel,
        out_shape=(sds, sds, sds),
        grid=(r // tm,),
        in_specs=[row, row, row, mod, mod, mod,
                  _resident((1, d)), _resident((d, d)), _resident((d, 2 * d))],
        out_specs=(row, row, row),
        compiler_params=_params(("arbitrary",)),
        name="att_out_rg_in",
    )(o, gate, x, mgate, shift, scale, g, w_out, w_in)


def _rg_gates(xc, wa_ref, ba_ref, wx_ref, bx_ref, lam_ref):
    xcb = xc.astype(BF16)
    w = RG_BLOCK_W
    ra = jnp.concatenate([_dot(xcb[:, n * w:(n + 1) * w], wa_ref[n]) for n in range(RG_BLOCKS)], axis=-1)
    rx = jnp.concatenate([_dot(xcb[:, n * w:(n + 1) * w], wx_ref[n]) for n in range(RG_BLOCKS)], axis=-1)
    r = _sigmoid(ra + ba_ref[...])
    ig = _sigmoid(rx + bx_ref[...])
    lam = lam_ref[...]
    log_sig_lam = -(jnp.maximum(-lam, 0.0) + jnp.log1p(jnp.exp(-jnp.abs(lam))))
    log_a = RG_C * r * log_sig_lam
    a = jnp.exp(log_a)
    gap = -jnp.tanh(log_a) * (a * a + 1.0)
    u = (gap * lax.rsqrt(jnp.maximum(gap, F32_TINY))) * (ig * xc)
    return a, u


def _rg_finish(y, gbr, x1, mgate, wout_ref, fg_ref):
    out = _dot((y * _silu(gbr)).astype(BF16), wout_ref[...])
    return _rmsnorm(x1 + mgate * out, fg_ref[...])


def _rg_prompt_kernel(xbr_ref, gbr_ref, x1_ref, mgate_ref, cw_ref, cb_ref, wa_ref, ba_ref,
                      wx_ref, bx_ref, lam_ref, wout_ref, fg_ref,
                      y_ref, tail_ref, hlast_ref, xx_s, a_s, u_s, h_s):
    tt = RG_TILE
    d = D_MODEL
    ti = pl.program_id(1)
    pad = SUBLANES

    @pl.when(ti == 0)
    def _():
        xx_s[0:pad, :] = jnp.zeros((pad, d), F32)
        h_s[...] = jnp.zeros((SUBLANES, d), F32)

    x = xbr_ref[...]
    xx_s[pad:pad + tt, :] = x
    xc = cb_ref[...] + cw_ref[0:1, :] * xx_s[pad - 3:pad - 3 + tt, :]
    xc = xc + cw_ref[1:2, :] * xx_s[pad - 2:pad - 2 + tt, :]
    xc = xc + cw_ref[2:3, :] * xx_s[pad - 1:pad - 1 + tt, :]
    xc = xc + cw_ref[3:4, :] * x
    xx_s[0:pad, :] = x[tt - pad:tt, :]

    a, u = _rg_gates(xc, wa_ref, ba_ref, wx_ref, bx_ref, lam_ref)
    a_s[...] = a
    u_s[...] = u
    sub = lax.broadcasted_iota(jnp.int32, (SUBLANES, d), 0)

    def group(i, h):
        r0 = pl.multiple_of(i * SUBLANES, SUBLANES)
        ag = a_s[pl.ds(r0, SUBLANES), :]
        ug = u_s[pl.ds(r0, SUBLANES), :]
        for s in (1, 2, 4):
            a_prev = jnp.where(sub >= s, pltpu.roll(ag, s, axis=0), 1.0)
            u_prev = jnp.where(sub >= s, pltpu.roll(ug, s, axis=0), 0.0)
            ug = ag * u_prev + ug
            ag = ag * a_prev
        hg = ag * h + ug
        u_s[pl.ds(r0, SUBLANES), :] = hg
        return jnp.broadcast_to(hg[SUBLANES - 1:SUBLANES, :], (SUBLANES, d))

    h = lax.fori_loop(0, tt // SUBLANES, group, h_s[...], unroll=4)
    h_s[...] = h

    y_ref[...] = _rg_finish(u_s[...], gbr_ref[...], x1_ref[...], mgate_ref[...], wout_ref, fg_ref)

    @pl.when(ti == pl.num_programs(1) - 1)
    def _():
        tail_ref[...] = x[tt - (CONV_W - 1):tt, :]
        hlast_ref[...] = h[0:1, :]


def _rg_prompt(xbr, gbr, x1, mgate, cw, cb, wa, ba, wx, bx, lam, wout, fg, *, batch, seq):
    d = D_MODEL
    tt = RG_TILE
    nt = seq // tt
    row = pl.BlockSpec((tt, d), lambda b, t: (b * nt + t, 0))
    vec = _resident((1, d))
    gates_w = _resident((RG_BLOCKS, RG_BLOCK_W, RG_BLOCK_W))
    return pl.pallas_call(
        _rg_prompt_kernel,
        out_shape=(jax.ShapeDtypeStruct((batch * seq, d), F32),
                   jax.ShapeDtypeStruct((batch, CONV_W - 1, d), F32),
                   jax.ShapeDtypeStruct((batch, 1, d), F32)),
        grid=(batch, nt),
        in_specs=[row, row, row,
                  pl.BlockSpec((None, 1, d), lambda b, t: (b, 0, 0)),
                  _resident((CONV_W, d)),
                  vec, gates_w, vec, gates_w, vec, vec,
                  _resident((d, d)),
                  vec],
        out_specs=(row,
                   pl.BlockSpec((None, CONV_W - 1, d), lambda b, t: (b, 0, 0)),
                   pl.BlockSpec((None, 1, d), lambda b, t: (b, 0, 0))),
        scratch_shapes=[pltpu.VMEM((tt + SUBLANES, d), F32),
                        pltpu.VMEM((tt, d), F32),
                        pltpu.VMEM((tt, d), F32),
                        pltpu.VMEM((SUBLANES, d), F32)],
        compiler_params=_params(("arbitrary", "arbitrary")),
        name="rg_prompt",
    )(xbr, gbr, x1, mgate, cw, cb, wa, ba, wx, bx, lam, wout, fg)


def _rg_sample_kernel(xbr_ref, gbr_ref, x1_ref, mgate_ref, sconv_ref, sh_ref, cw_ref, cb_ref,
                      wa_ref, ba_ref, wx_ref, bx_ref, lam_ref, wout_ref, fg_ref,
                      y_ref, tail_ref, h_ref):
    x = xbr_ref[...]
    xc = cb_ref[...] + cw_ref[0:1, :] * sconv_ref[0]
    xc = xc + cw_ref[1:2, :] * sconv_ref[1]
    xc = xc + cw_ref[2:3, :] * sconv_ref[2]
    xc = xc + cw_ref[3:4, :] * x
    a, u = _rg_gates(xc, wa_ref, ba_ref, wx_ref, bx_ref, lam_ref)
    h = a * sh_ref[...] + u
    h_ref[...] = h
    tail_ref[0] = sconv_ref[1]
    tail_ref[1] = sconv_ref[2]
    tail_ref[2] = x
    y_ref[...] = _rg_finish(h, gbr_ref[...], x1_ref[...], mgate_ref[...], wout_ref, fg_ref)


def _rg_sample(xbr, gbr, x1, mgate, sconv, sh, cw, cb, wa, ba, wx, bx, lam, wout, fg):
    nb, d = xbr.shape
    sds = jax.ShapeDtypeStruct((nb, d), F32)
    return pl.pallas_call(
        _rg_sample_kernel,
        out_shape=(sds, jax.ShapeDtypeStruct((CONV_W - 1, nb, d), F32), sds),
        compiler_params=pltpu.CompilerParams(vmem_limit_bytes=VMEM_LIMIT),
        name="rg_sample",
    )(xbr, gbr, x1, mgate, sconv, sh, cw, cb, wa, ba, wx, bx, lam, wout, fg)


def kernel(x_prompt, x_sample, c_prompt, c_sample, cache_k, cache_v, page_table, state_conv, state_h,
           norm_g, mod_w, mod_b, att_w_in, att_bias, att_w_out, rg_w_in, rg_conv_w, rg_conv_b,
           rg_w_a, rg_b_a, rg_w_x, rg_b_x, rg_lambda, rg_w_out, final_g):
    batch, seq, d = x_prompt.shape
    nb, dec_seq, _ = x_sample.shape
    assert d == D_MODEL and dec_seq == 1 and mod_w.shape[0] == 2

    mod = _modulation(jnp.concatenate([c_prompt, c_sample], axis=0), mod_w, mod_b)

    def mod_parts(layer, lo, hi, shape):
        m = mod[layer, lo:hi]
        return tuple(m[:, i * d:(i + 1) * d].reshape(shape) for i in range(3))

    shift0p, scale0p, gate0p = mod_parts(0, 0, batch, (batch, 1, d))
    shift1p, scale1p, gate1p = mod_parts(1, 0, batch, (batch, 1, d))
    shift0s, scale0s, gate0s = mod_parts(0, batch, batch + nb, (nb, d))
    shift1s, scale1s, gate1s = mod_parts(1, batch, batch + nb, (nb, d))

    g0, g1 = norm_g[0].reshape(1, d), norm_g[1].reshape(1, d)
    w_att_in = att_w_in[0].astype(BF16)
    w_kv_t = jnp.swapaxes(att_w_in[0][:, d:3 * d], 0, 1).astype(BF16)
    w_att_out = att_w_out[0].astype(BF16)
    w_rg_in = rg_w_in[0].astype(BF16)
    w_rg_out = rg_w_out[0].astype(BF16)
    wa, wx = rg_w_a[0].astype(BF16), rg_w_x[0].astype(BF16)
    vec = lambda v: v.reshape(1, d)
    rg_consts = (rg_conv_w[0], vec(rg_conv_b[0]), wa, vec(rg_b_a[0]), wx, vec(rg_b_x[0]),
                 vec(rg_lambda[0]), w_rg_out, vec(final_g))
    bias = att_bias[0]

    xp = x_prompt.reshape(batch * seq, d)
    xs = x_sample.reshape(nb, d)
    q_p, kt_p, vt_p, gate_p, kbt_p, vb_p = _att_in_prompt(xp, shift0p, scale0p, g0, w_att_in, w_kv_t,
                                                          batch=batch, seq=seq)
    q_s, k_s, v_s, gate_s = _att_in_sample(xs, shift0s, scale0s, g0, w_att_in)
    pages_t = lambda cache: jnp.transpose(cache, (0, 1, 3, 4, 2)).reshape(-1, d, PAGE)
    o_p, o_s = _sb_attention(page_table, bias, q_p, kbt_p, vb_p, q_s, k_s, v_s,
                             pages_t(cache_k), pages_t(cache_v), batch=batch, seq=seq)

    x1_p, xbr_p, gbr_p = _att_out_rg_in(o_p, gate_p, xp, gate0p, shift1p, scale1p, g1, w_att_out, w_rg_in,
                                        tm=ROW_TILE, rows_per_mod=seq)
    y_p, tail_p, hlast_p = _rg_prompt(xbr_p, gbr_p, x1_p, gate1p, *rg_consts, batch=batch, seq=seq)
    as3 = lambda m: m.reshape(1, nb, d)
    x1_s, xbr_s, gbr_s = _att_out_rg_in(o_s, gate_s, xs, as3(gate0s), as3(shift1s), as3(scale1s), g1,
                                        w_att_out, w_rg_in, tm=nb, rows_per_mod=nb)
    y_s, tail_s, h_s = _rg_sample(xbr_s, gbr_s, x1_s, gate1s,
                                  jnp.swapaxes(state_conv[0], 0, 1), state_h[0], *rg_consts)

    def kv_rows(t):
        return jnp.transpose(t.reshape(1, batch, HEADS, HEAD_DIM, seq), (0, 1, 4, 2, 3))

    kv_s = (1, nb, 1, HEADS, HEAD_DIM)
    return (y_p.reshape(batch, seq, d), y_s.reshape(nb, 1, d),
            kv_rows(kt_p), kv_rows(vt_p), k_s.reshape(kv_s), v_s.reshape(kv_s),
            tail_p.reshape(1, batch, CONV_W - 1, d), hlast_p.reshape(1, batch, d),
            jnp.swapaxes(tail_s, 0, 1).reshape(1, nb, CONV_W - 1, d), h_s.reshape(1, nb, d))
```

```python
import functools

import jax
import jax.numpy as jnp
from jax import lax
from jax.experimental import pallas as pl
from jax.experimental.pallas import tpu as pltpu

F32 = jnp.float32
BF16 = jnp.bfloat16

D_MODEL = 1024
HEADS = 16
HEAD_DIM = 64
LANES = 128
SUBLANES = 8
HEADS_PER_LANE_TILE = LANES // HEAD_DIM
PAGE = 128
RG_BLOCKS = 4
RG_BLOCK_W = D_MODEL // RG_BLOCKS
CONV_W = 4
RG_C = 8.0
EPS = 1e-6
Q_SCALE = HEAD_DIM ** -0.5
LOG2_E = 1.4426950408889634
F32_TINY = 1.1754943508222875e-38

ROW_TILE = 512
ATT_TILE = 256
RG_TILE = 256
PAGES_PER_STEP = 8
CHUNKS_PER_STEP = 2
VMEM_LIMIT = 56 * 1024 * 1024

NT_DIMS = (((1,), (1,)), ((), ()))


def _dot(a, b):
    return jnp.dot(a, b, preferred_element_type=F32)


def _dot_nt(a, b):
    return lax.dot_general(a, b, NT_DIMS, preferred_element_type=F32)


def _sigmoid(x):
    return 0.5 * jnp.tanh(0.5 * x) + 0.5


def _silu(x):
    return x * _sigmoid(x)


def _neg_softplus(z):
    return -(jnp.maximum(z, 0.0) + jnp.log(1.0 + jnp.exp(-jnp.abs(z))))


def _split2(x):
    hi = x.astype(BF16)
    lo = (x - hi.astype(F32)).astype(BF16)
    return hi, lo


def _split3(x):
    hi = x.astype(BF16)
    r = x - hi.astype(F32)
    mid = r.astype(BF16)
    lo = (r - mid.astype(F32)).astype(BF16)
    return hi, mid, lo


def _dot3(x, m):
    hi, mid, lo = _split3(x)
    return _dot(hi, m) + _dot(mid, m) + _dot(lo, m)


def _rmsnorm(x, g):
    ms = jnp.mean(x * x, axis=-1, keepdims=True)
    return (x * lax.rsqrt(ms + EPS)) * g


def _params(semantics):
    return pltpu.CompilerParams(dimension_semantics=semantics, vmem_limit_bytes=VMEM_LIMIT)


def _resident(shape):
    zeros = (0,) * len(shape)
    return pl.BlockSpec(shape, lambda *_: zeros, pipeline_mode=pl.Buffered(1))


def _mod_kernel(c_ref, w_ref, b_ref, o_ref):
    c = c_ref[...]
    o_ref[...] = _dot(_silu(c).astype(BF16), w_ref[...].astype(BF16)) + b_ref[...]


def _modulation(c_all, mod_w, mod_b):
    depth, d, n = mod_w.shape
    rows = c_all.shape[0]
    tn = 1024
    return pl.pallas_call(
        _mod_kernel,
        out_shape=jax.ShapeDtypeStruct((depth, rows, n), F32),
        grid=(depth, n // tn),
        in_specs=[
            pl.BlockSpec((rows, d), lambda l, j: (0, 0)),
            pl.BlockSpec((None, d, tn), lambda l, j: (l, 0, j)),
            pl.BlockSpec((None, 1, tn), lambda l, j: (l, 0, j)),
        ],
        out_specs=pl.BlockSpec((None, rows, tn), lambda l, j: (l, 0, j)),
        compiler_params=_params(("arbitrary", "arbitrary")),
        name="modulation",
    )(c_all, mod_w, mod_b.reshape(depth, 1, n))


def _att_in_hidden(x_ref, shift_ref, scale_ref, g_ref):
    h = _rmsnorm(x_ref[...], g_ref[...]) * (1.0 + scale_ref[...]) + shift_ref[...]
    return h.astype(BF16)


def _att_in_prompt_kernel(x_ref, shift_ref, scale_ref, g_ref, w_ref, wkv_t_ref,
                          q_ref, kt_ref, vt_ref, gate_ref, kbt_ref, vb_ref):
    d = D_MODEL
    t = ATT_TILE
    h = _att_in_hidden(x_ref, shift_ref, scale_ref, g_ref)
    q_ref[...] = (_dot(h, w_ref[:, 0:d]) * Q_SCALE).astype(BF16)
    kt = _dot_nt(wkv_t_ref[0:d, :], h)
    kt_ref[...] = kt
    for j in range(kbt_ref.shape[0]):
        kbt_ref[j] = kt[:, j * t:(j + 1) * t].astype(BF16)
    vt_ref[...] = _dot_nt(wkv_t_ref[d:2 * d, :], h)
    vb_ref[...] = _dot(h, w_ref[:, 2 * d:3 * d]).astype(BF16)
    gate_ref[...] = _dot(h, w_ref[:, 3 * d:4 * d])


def _att_in_prompt(x, shift, scale, g, w, wkv_t, *, batch, seq):
    r, d = x.shape
    tm = ROW_TILE
    t = ATT_TILE
    per_b = seq // tm
    row = pl.BlockSpec((tm, d), lambda i: (i, 0))
    mod = pl.BlockSpec((None, 1, d), lambda i: (i // per_b, 0, 0))
    feat = pl.BlockSpec((None, d, tm), lambda i: (i // per_b, 0, i % per_b))
    feat_tiles = pl.BlockSpec((None, tm // t, d, t), lambda i: (i // per_b, i % per_b, 0, 0))
    return pl.pallas_call(
        _att_in_prompt_kernel,
        out_shape=(
            jax.ShapeDtypeStruct((r, d), BF16),
            jax.ShapeDtypeStruct((batch, d, seq), F32),
            jax.ShapeDtypeStruct((batch, d, seq), F32),
            jax.ShapeDtypeStruct((r, d), F32),
            jax.ShapeDtypeStruct((batch, seq // t, d, t), BF16),
            jax.ShapeDtypeStruct((r, d), BF16),
        ),
        grid=(r // tm,),
        in_specs=[row, mod, mod, _resident((1, d)), _resident((d, 4 * d)), _resident((2 * d, d))],
        out_specs=(row, feat, feat, row, feat_tiles, row),
        compiler_params=_params(("arbitrary",)),
        name="att_in_prompt",
    )(x, shift, scale, g, w, wkv_t)


def _att_in_sample_kernel(x_ref, shift_ref, scale_ref, g_ref, w_ref, q_ref, k_ref, v_ref, gate_ref):
    d = D_MODEL
    h = _att_in_hidden(x_ref, shift_ref, scale_ref, g_ref)
    q_ref[...] = (_dot(h, w_ref[:, 0:d]) * Q_SCALE).astype(BF16)
    k_ref[...] = _dot(h, w_ref[:, d:2 * d])
    v_ref[...] = _dot(h, w_ref[:, 2 * d:3 * d])
    gate_ref[...] = _dot(h, w_ref[:, 3 * d:4 * d])


def _att_in_sample(x, shift, scale, g, w):
    r, d = x.shape
    sds = jax.ShapeDtypeStruct((r, d), F32)
    return pl.pallas_call(
        _att_in_sample_kernel,
        out_shape=(jax.ShapeDtypeStruct((r, d), BF16), sds, sds, sds),
        compiler_params=pltpu.CompilerParams(vmem_limit_bytes=VMEM_LIMIT),
        name="att_in_sample",
    )(x, shift, scale, g, w)


MASKED_LOG = -1e30


def _sb_scores(nqs, biases, k, masks):
    t = ATT_TILE
    nz_all = _dot(jnp.concatenate(nqs, axis=0), k)
    log_betas, splits, totals = [], [], []
    for c in range(len(nqs)):
        nz = nz_all[c * t:(c + 1) * t] - biases[c]
        s = jnp.log(1.0 + jnp.exp2(jnp.abs(nz) * (-LOG2_E)))
        lk = jnp.minimum(nz, 0.0) - s
        lb = lk - nz
        if masks[c] is not None:
            lk = jnp.where(masks[c], lk, 0.0)
            lb = jnp.where(masks[c], lb, MASKED_LOG)
        log_betas.append(lb)
        splits.append(jnp.concatenate(_split2(lk), axis=1))
        totals.append(jnp.sum(lk, axis=-1, keepdims=True))
    return log_betas, splits, totals


def _sb_apply(scores, v, u2, carries, accs):
    t = ATT_TILE
    log_betas, splits, totals = scores
    n = len(log_betas)
    later_all = _dot(jnp.concatenate(splits, axis=0), u2)
    weights = [jnp.exp(log_betas[c] + later_all[c * t:(c + 1) * t] + carries[c]).astype(BF16)
               for c in range(n)]
    av_all = _dot(jnp.concatenate(weights, axis=0), v)
    accs = [accs[c] + av_all[c * t:(c + 1) * t] for c in range(n)]
    carries = [carries[c] + totals[c] for c in range(n)]
    return carries, accs


def _sb_prompt_step(bias_ref, q_ref, kt_ref, v_ref, u2_ref, o_ref, hp, i):
    t = ATT_TILE
    lane = lax.broadcasted_iota(jnp.int32, (1, LANES), 1)
    row = lax.broadcasted_iota(jnp.int32, (t, t), 0)
    col = lax.broadcasted_iota(jnp.int32, (t, t), 1)
    causal = col < row
    nq = -q_ref[...].astype(F32)
    u2 = u2_ref[...]

    def values(j):
        return v_ref[pl.ds(pl.multiple_of(j * t, t), t), :]

    nqs, biases = [], []
    for hh in range(HEADS_PER_LANE_TILE):
        in_head = (lane >= hh * HEAD_DIM) & (lane < (hh + 1) * HEAD_DIM)
        masked = jnp.where(in_head, nq, 0.0).astype(BF16)
        nqs += [masked[0:t], masked[t:2 * t]]
        biases += [bias_ref[hp * HEADS_PER_LANE_TILE + hh]] * 2
    n = len(nqs)
    upper = [c for c in range(n) if c % 2 == 1]
    pick = lambda xs: [xs[c] for c in upper]

    carries = [jnp.zeros((t, 1), F32)] * n
    accs = [jnp.zeros((t, LANES), F32)] * n
    first = _sb_scores(pick(nqs), pick(biases), kt_ref[2 * i + 1], [causal] * len(upper))
    second = _sb_scores(nqs, biases, kt_ref[2 * i], [None if c in upper else causal for c in range(n)])
    cu, au = _sb_apply(first, values(2 * i + 1), u2, pick(carries), pick(accs))
    for idx, c in enumerate(upper):
        carries[c], accs[c] = cu[idx], au[idx]
    carries, accs = _sb_apply(second, values(2 * i), u2, carries, accs)

    def body(it, st):
        j = 2 * (i - it) - 1
        first = _sb_scores(nqs, biases, kt_ref[j], [None] * n)
        second = _sb_scores(nqs, biases, kt_ref[j - 1], [None] * n)
        cs, as_ = _sb_apply(first, values(j), u2, list(st[0]), list(st[1]))
        cs, as_ = _sb_apply(second, values(j - 1), u2, cs, as_)
        return tuple(cs), tuple(as_)

    carries, accs = lax.fori_loop(0, i, body, (tuple(carries), tuple(accs)))
    for r in range(2):
        o_ref[r * t:(r + 1) * t, :] = jnp.where(lane < HEAD_DIM, accs[r], accs[2 + r])


def _sb_sample_chunk(k_refs, v_refs, bias_ref, u_ref, ones, qbd_ref, acc_ref, carry_ref, a_ref):
    g_n = len(k_refs)
    hd = HEAD_DIM
    qbd = qbd_ref[...]
    z = jnp.concatenate([_dot(qbd, k_refs[g][...].astype(BF16)) for g in range(g_n)], axis=0)
    z = z + bias_ref[...]
    lk = _neg_softplus(z)
    hi, mid, lo = _split3(lk)
    u = u_ref[...]
    later = _dot(hi, u) + _dot(mid, u) + _dot(lo, u)
    tot = _dot(hi, ones) + _dot(mid, ones) + _dot(lo, ones)
    ex = z + lk + later
    off = carry_ref[...]
    for g in reversed(range(g_n)):
        sl = slice(g * HEADS, (g + 1) * HEADS)
        a_ref[sl, :] = jnp.exp(ex[sl] + off)
        off = off + tot[sl]
    carry_ref[...] = off
    for h in range(HEADS):
        feat = slice(h * hd, (h + 1) * hd)
        acc = acc_ref[feat, :]
        for g in range(g_n):
            acc = acc + a_ref[g * HEADS + h:g * HEADS + h + 1, :] * v_refs[g][feat, :]
        acc_ref[feat, :] = acc


def _sb_sample_step(first, last, q_ref, kn_ref, vn_ref, bias_ref, bias_row_ref, u_ref, ones_ref, hsel_ref,
                    k_chunks, v_chunks, o_ref, qbd_ref, acc_ref, carry_ref, a_ref, *, beside):
    d = D_MODEL
    ones = ones_ref[...]

    def new_token():
        mask_new = jnp.zeros((SUBLANES, LANES), jnp.int32) < jnp.zeros((SUBLANES, LANES), jnp.int32)
        qk = q_ref[...].astype(F32) * kn_ref[...].astype(BF16).astype(F32)
        z_new = _dot3(jnp.broadcast_to(qk, (SUBLANES, d)), hsel_ref[...]) + bias_row_ref[...]
        lk_new = _neg_softplus(z_new)
        a_new = jnp.where(mask_new, jnp.exp(z_new + lk_new), 0.0)
        return a_new, jnp.where(mask_new, lk_new, 0.0)

    @pl.when(first)
    def _():
        head = lax.broadcasted_iota(jnp.int32, (HEADS, d), 0)
        feat = lax.broadcasted_iota(jnp.int32, (HEADS, d), 1)
        own = (feat >= head * HEAD_DIM) & (feat < (head + 1) * HEAD_DIM)
        qbd_ref[...] = jnp.where(own, jnp.broadcast_to(q_ref[...].astype(F32), (HEADS, d)), 0.0).astype(BF16)
        acc_ref[...] = jnp.zeros((d, PAGE), F32)
        _, lk_new = new_token()
        head_diag = (lax.broadcasted_iota(jnp.int32, (HEADS, LANES), 0)
                     == lax.broadcasted_iota(jnp.int32, (HEADS, LANES), 1))
        carry_ref[...] = _dot3(jnp.where(head_diag, jnp.broadcast_to(lk_new[0:1, :], (HEADS, LANES)), 0.0), ones)

    for k_refs, v_refs in zip(k_chunks, v_chunks):
        _sb_sample_chunk(k_refs, v_refs, bias_ref, u_ref, ones, qbd_ref, acc_ref, carry_ref, a_ref)
    beside()

    @pl.when(last)
    def _():
        hi_a, mid_a, lo_a = _split3(acc_ref[...])
        ones8 = jnp.ones((SUBLANES, PAGE), BF16)
        o = _dot_nt(ones8, hi_a) + _dot_nt(ones8, mid_a) + _dot_nt(ones8, lo_a)
        a_new, _ = new_token()
        o_new = _dot_nt(a_new.astype(BF16), hsel_ref[...]) * vn_ref[...]
        o_ref[...] = o[0:1, :] + o_new[0:1, :]


def _sb_attention_kernel(step_pages_ref, bias_ref, q_ref, kt_ref, v_ref, u2_ref,
                         qs_ref, kn_ref, vn_ref, bias_col_ref, bias_row_ref, u_ref, ones_ref, hsel_ref, *rest,
                         per_seq):
    g_n, n_chunks = PAGES_PER_STEP, CHUNKS_PER_STEP
    n = g_n * n_chunks
    chunks = lambda refs: [refs[c * g_n:(c + 1) * g_n] for c in range(n_chunks)]
    k_chunks, v_chunks = chunks(rest[:n]), chunks(rest[n:2 * n])
    o_ref, os_ref = rest[2 * n:2 * n + 2]
    scratch = rest[2 * n + 2:]
    b, hp, i = pl.program_id(0), pl.program_id(1), pl.program_id(2)
    step = (b * pl.num_programs(1) + hp) * pl.num_programs(2) + i
    part = step & (per_seq - 1)
    prompt = functools.partial(_sb_prompt_step, bias_ref, q_ref, kt_ref, v_ref, u2_ref, o_ref, hp, i)
    _sb_sample_step(part == 0, part == per_seq - 1, qs_ref, kn_ref, vn_ref, bias_col_ref, bias_row_ref,
                    u_ref, ones_ref, hsel_ref, k_chunks, v_chunks, os_ref, *scratch, beside=prompt)


def _sb_attention(page_table, bias, q, kbt, vb, q_s, k_new, v_new, cache_kt, cache_vt, *, batch, seq):
    d = q.shape[1]
    t = ATT_TILE
    nq = seq // t
    steps = nq // 2
    nb = q_s.shape[0]
    g_n, n_chunks = PAGES_PER_STEP, CHUNKS_PER_STEP
    nc = page_table.shape[1] // g_n
    per_seq = nc // n_chunks
    heads2 = d // LANES
    assert batch * heads2 * steps == nb * per_seq
    i32 = jnp.int32

    idx = jnp.arange(t, dtype=i32)
    u_t = (idx[:, None] > idx[None, :]).astype(BF16)
    u2 = jnp.concatenate([u_t, u_t], axis=0)
    key = jnp.arange(PAGE, dtype=i32)
    u = (key[:, None] > key[None, :]).astype(BF16)
    ones = jnp.ones((LANES, LANES), BF16)
    feat_head = jnp.arange(d, dtype=i32) // HEAD_DIM
    hsel = (feat_head[:, None] == jnp.arange(LANES, dtype=i32)[None, :]).astype(BF16)
    bias_col = jnp.broadcast_to(jnp.tile(bias, g_n)[:, None], (g_n * HEADS, PAGE))
    bias_row = jnp.pad(bias, (0, LANES - HEADS)).reshape(1, LANES)

    n = g_n * n_chunks
    n_steps = batch * heads2 * steps
    s_idx = jnp.arange(n_steps, dtype=i32)
    slot = jnp.arange(n, dtype=i32)
    chunk = (s_idx % per_seq)[:, None] * n_chunks + (slot // g_n)[None, :]
    step_pages = page_table[(s_idx // per_seq)[:, None], (nc - 1 - chunk) * g_n + (slot % g_n)[None, :]]
    step_pages = step_pages.reshape(n_steps * n).astype(i32)
    log2_per_seq = per_seq.bit_length() - 1
    assert per_seq == 1 << log2_per_seq

    def step_of(b, h, i):
        return (b * heads2 + h) * steps + i

    def page_spec(k):
        return pl.BlockSpec((None, d, PAGE), lambda b, h, i, sp, _: (sp[step_of(b, h, i) * n + k], 0, 0))

    const2 = lambda b, h, i, sp, _: (0, 0)
    row3 = pl.BlockSpec(
        (None, 1, d), lambda b, h, i, sp, _: (lax.shift_right_logical(step_of(b, h, i), log2_per_seq), 0, 0))
    pages = [page_spec(k) for k in range(n)]
    grid_spec = pltpu.PrefetchScalarGridSpec(
        num_scalar_prefetch=2,
        grid=(batch, heads2, steps),
        in_specs=[pl.BlockSpec((2 * t, LANES), lambda b, h, i, pt, _: (b * steps + i, h)),
                  pl.BlockSpec((None, nq, LANES, t), lambda b, h, i, pt, _: (b, 0, h, 0)),
                  pl.BlockSpec((seq, LANES), lambda b, h, i, pt, _: (b, h)),
                  pl.BlockSpec((2 * t, t), const2),
                  row3, row3, row3,
                  pl.BlockSpec((g_n * HEADS, PAGE), const2),
                  pl.BlockSpec((1, LANES), const2),
                  pl.BlockSpec((PAGE, PAGE), const2),
                  pl.BlockSpec((LANES, LANES), const2),
                  pl.BlockSpec((d, LANES), const2)]
                 + pages * 2,
        out_specs=(pl.BlockSpec((2 * t, LANES), lambda b, h, i, pt, _: (b * steps + i, h)), row3),
        scratch_shapes=[pltpu.VMEM((HEADS, d), BF16),
                        pltpu.VMEM((d, PAGE), F32),
                        pltpu.VMEM((HEADS, PAGE), F32),
                        pltpu.VMEM((g_n * HEADS, PAGE), F32)],
    )
    o_p, o_s = pl.pallas_call(
        functools.partial(_sb_attention_kernel, per_seq=per_seq),
        out_shape=(jax.ShapeDtypeStruct((batch * seq, d), F32), jax.ShapeDtypeStruct((nb, 1, d), F32)),
        grid_spec=grid_spec,
        compiler_params=_params(("arbitrary", "arbitrary", "arbitrary")),
        name="sb_attention",
    )(step_pages, bias, q, kbt, vb, u2,
      q_s.reshape(nb, 1, d), k_new.reshape(nb, 1, d), v_new.reshape(nb, 1, d),
      bias_col, bias_row, u, ones, hsel, *([cache_kt] * n), *([cache_vt] * n))
    return o_p, o_s.reshape(nb, d)


def _att_out_rg_in(o, gate, x, mgate, shift, scale, g, wout_ref, win_ref):
    d = D_MODEL
    y = _dot((o * _silu(gate)).astype(BF16), wout_ref[...])
    x1 = x + mgate * y
    h = (_rmsnorm(x1, g) * (1.0 + scale) + shift).astype(BF16)
    return x1, _dot(h, win_ref[:, 0:d]), _dot(h, win_ref[:, d:2 * d])


def _rg_gates(xc, wa_ref, ba_ref, wx_ref, bx_ref, lam_ref):
    xcb = xc.astype(BF16)
    w = RG_BLOCK_W
    ra = jnp.concatenate([_dot(xcb[:, n * w:(n + 1) * w], wa_ref[n]) for n in range(RG_BLOCKS)], axis=-1)
    rx = jnp.concatenate([_dot(xcb[:, n * w:(n + 1) * w], wx_ref[n]) for n in range(RG_BLOCKS)], axis=-1)
    r = _sigmoid(ra + ba_ref[...])
    ig = _sigmoid(rx + bx_ref[...])
    lam = lam_ref[...]
    log_sig_lam = -(jnp.maximum(-lam, 0.0) + jnp.log1p(jnp.exp(-jnp.abs(lam))))
    log_a = RG_C * r * log_sig_lam
    a = jnp.exp(log_a)
    gap = -jnp.tanh(log_a) * (a * a + 1.0)
    u = (gap * lax.rsqrt(jnp.maximum(gap, F32_TINY))) * (ig * xc)
    return a, u


def _rg_finish(y, gbr, x1, mgate, wout_ref, fg_ref):
    out = _dot((y * _silu(gbr)).astype(BF16), wout_ref[...])
    return _rmsnorm(x1 + mgate * out, fg_ref[...])


def _rg_prompt_kernel(o_ref, gate_ref, xin_ref, mgate0_ref, shift_ref, scale_ref, g_ref, wao_ref, win_ref,
                      mgate_ref, cw_ref, cb_ref, wa_ref, ba_ref, wx_ref, bx_ref, lam_ref, wout_ref, fg_ref,
                      y_ref, tail_ref, hlast_ref, hist_s, a_s, u_s, h_s):
    tt = RG_TILE
    d = D_MODEL
    ti = pl.program_id(1)
    pad = SUBLANES

    @pl.when(ti == 0)
    def _():
        hist_s[...] = jnp.zeros((pad, d), F32)
        h_s[...] = jnp.zeros((SUBLANES, d), F32)

    x1, x, gbr = _att_out_rg_in(o_ref[...], gate_ref[...], xin_ref[...], mgate0_ref[...], shift_ref[...],
                                scale_ref[...], g_ref[...], wao_ref, win_ref)
    full = jnp.concatenate([hist_s[...], x], axis=0)
    xc = cb_ref[...] + cw_ref[3:4, :] * x
    for k in range(1, CONV_W):
        xc = xc + cw_ref[CONV_W - 1 - k:CONV_W - k, :] * pltpu.roll(full, k, axis=0)[pad:pad + tt, :]
    hist_s[...] = x[tt - pad:tt, :]

    a, u = _rg_gates(xc, wa_ref, ba_ref, wx_ref, bx_ref, lam_ref)
    a_s[...] = a
    u_s[...] = u
    sub = lax.broadcasted_iota(jnp.int32, (SUBLANES, d), 0)

    def group(i, h):
        r0 = pl.multiple_of(i * SUBLANES, SUBLANES)
        ag = a_s[pl.ds(r0, SUBLANES), :]
        ug = u_s[pl.ds(r0, SUBLANES), :]
        for s in (1, 2, 4):
            a_prev = jnp.where(sub >= s, pltpu.roll(ag, s, axis=0), 1.0)
            u_prev = jnp.where(sub >= s, pltpu.roll(ug, s, axis=0), 0.0)
            ug = ag * u_prev + ug
            ag = ag * a_prev
        hg = ag * h + ug
        u_s[pl.ds(r0, SUBLANES), :] = hg
        return jnp.broadcast_to(hg[SUBLANES - 1:SUBLANES, :], (SUBLANES, d))

    h = lax.fori_loop(0, tt // SUBLANES, group, h_s[...], unroll=4)
    h_s[...] = h

    y_ref[...] = _rg_finish(u_s[...], gbr, x1, mgate_ref[...], wout_ref, fg_ref)

    @pl.when(ti == pl.num_programs(1) - 1)
    def _():
        tail_ref[...] = x[tt - (CONV_W - 1):tt, :]
        hlast_ref[...] = h[0:1, :]


def _rg_prompt(o, gate, x, mgate0, shift, scale, g, w_att_out, w_rg_in,
               mgate, cw, cb, wa, ba, wx, bx, lam, wout, fg, *, batch, seq):
    d = D_MODEL
    tt = RG_TILE
    nt = seq // tt
    row = pl.BlockSpec((tt, d), lambda b, t: (b * nt + t, 0))
    per_seq = pl.BlockSpec((None, 1, d), lambda b, t: (b, 0, 0))
    vec = _resident((1, d))
    gates_w = _resident((RG_BLOCKS, RG_BLOCK_W, RG_BLOCK_W))
    return pl.pallas_call(
        _rg_prompt_kernel,
        out_shape=(jax.ShapeDtypeStruct((batch * seq, d), F32),
                   jax.ShapeDtypeStruct((batch, CONV_W - 1, d), F32),
                   jax.ShapeDtypeStruct((batch, 1, d), F32)),
        grid=(batch, nt),
        in_specs=[row, row, row, per_seq, per_seq, per_seq, vec, _resident((d, d)), _resident((d, 2 * d)),
                  per_seq,
                  _resident((CONV_W, d)),
                  vec, gates_w, vec, gates_w, vec, vec,
                  _resident((d, d)),
                  vec],
        out_specs=(row,
                   pl.BlockSpec((None, CONV_W - 1, d), lambda b, t: (b, 0, 0)),
                   per_seq),
        scratch_shapes=[pltpu.VMEM((SUBLANES, d), F32),
                        pltpu.VMEM((tt, d), F32),
                        pltpu.VMEM((tt, d), F32),
                        pltpu.VMEM((SUBLANES, d), F32)],
        compiler_params=_params(("arbitrary", "arbitrary")),
        name="rg_prompt",
    )(o, gate, x, mgate0, shift, scale, g, w_att_out, w_rg_in, mgate, cw, cb, wa, ba, wx, bx, lam, wout, fg)


def _rg_sample_kernel(o_ref, gate_ref, xin_ref, mgate0_ref, shift_ref, scale_ref, g_ref, wao_ref, win_ref,
                      mgate_ref, sconv_ref, sh_ref, cw_ref, cb_ref,
                      wa_ref, ba_ref, wx_ref, bx_ref, lam_ref, wout_ref, fg_ref,
                      y_ref, tail_ref, h_ref):
    x1, x, gbr = _att_out_rg_in(o_ref[...], gate_ref[...], xin_ref[...], mgate0_ref[...], shift_ref[...],
                                scale_ref[...], g_ref[...], wao_ref, win_ref)
    xc = cb_ref[...] + cw_ref[0:1, :] * sconv_ref[0]
    xc = xc + cw_ref[1:2, :] * sconv_ref[1]
    xc = xc + cw_ref[2:3, :] * sconv_ref[2]
    xc = xc + cw_ref[3:4, :] * x
    a, u = _rg_gates(xc, wa_ref, ba_ref, wx_ref, bx_ref, lam_ref)
    h = a * sh_ref[...] + u
    h_ref[...] = h
    tail_ref[0] = sconv_ref[1]
    tail_ref[1] = sconv_ref[2]
    tail_ref[2] = x
    y_ref[...] = _rg_finish(h, gbr, x1, mgate_ref[...], wout_ref, fg_ref)


def _rg_sample(o, gate, x, mgate0, shift, scale, g, w_att_out, w_rg_in,
               mgate, sconv, sh, cw, cb, wa, ba, wx, bx, lam, wout, fg):
    nb, d = x.shape
    sds = jax.ShapeDtypeStruct((nb, d), F32)
    return pl.pallas_call(
        _rg_sample_kernel,
        out_shape=(sds, jax.ShapeDtypeStruct((CONV_W - 1, nb, d), F32), sds),
        compiler_params=pltpu.CompilerParams(vmem_limit_bytes=VMEM_LIMIT),
        name="rg_sample",
    )(o, gate, x, mgate0, shift, scale, g, w_att_out, w_rg_in, mgate, sconv, sh,
      cw, cb, wa, ba, wx, bx, lam, wout, fg)


def kernel(x_prompt, x_sample, c_prompt, c_sample, cache_k, cache_v, page_table, state_conv, state_h,
           norm_g, mod_w, mod_b, att_w_in, att_bias, att_w_out, rg_w_in, rg_conv_w, rg_conv_b,
           rg_w_a, rg_b_a, rg_w_x, rg_b_x, rg_lambda, rg_w_out, final_g):
    batch, seq, d = x_prompt.shape
    nb, dec_seq, _ = x_sample.shape
    assert d == D_MODEL and dec_seq == 1 and mod_w.shape[0] == 2

    mod = _modulation(jnp.concatenate([c_prompt, c_sample], axis=0), mod_w, mod_b)

    def mod_parts(layer, lo, hi, shape):
        m = mod[layer, lo:hi]
        return tuple(m[:, i * d:(i + 1) * d].reshape(shape) for i in range(3))

    shift0p, scale0p, gate0p = mod_parts(0, 0, batch, (batch, 1, d))
    shift1p, scale1p, gate1p = mod_parts(1, 0, batch, (batch, 1, d))
    shift0s, scale0s, gate0s = mod_parts(0, batch, batch + nb, (nb, d))
    shift1s, scale1s, gate1s = mod_parts(1, batch, batch + nb, (nb, d))

    g0, g1 = norm_g[0].reshape(1, d), norm_g[1].reshape(1, d)
    w_att_in = att_w_in[0].astype(BF16)
    w_kv_t = jnp.swapaxes(att_w_in[0][:, d:3 * d], 0, 1).astype(BF16)
    w_att_out = att_w_out[0].astype(BF16)
    w_rg_in = rg_w_in[0].astype(BF16)
    w_rg_out = rg_w_out[0].astype(BF16)
    wa, wx = rg_w_a[0].astype(BF16), rg_w_x[0].astype(BF16)
    vec = lambda v: v.reshape(1, d)
    rg_consts = (rg_conv_w[0], vec(rg_conv_b[0]), wa, vec(rg_b_a[0]), wx, vec(rg_b_x[0]),
                 vec(rg_lambda[0]), w_rg_out, vec(final_g))
    bias = att_bias[0]

    xp = x_prompt.reshape(batch * seq, d)
    xs = x_sample.reshape(nb, d)
    q_p, kt_p, vt_p, gate_p, kbt_p, vb_p = _att_in_prompt(xp, shift0p, scale0p, g0, w_att_in, w_kv_t,
                                                          batch=batch, seq=seq)
    q_s, k_s, v_s, gate_s = _att_in_sample(xs, shift0s, scale0s, g0, w_att_in)
    pages_t = lambda cache: jnp.transpose(cache, (0, 1, 3, 4, 2)).reshape(-1, d, PAGE)
    o_p, o_s = _sb_attention(page_table, bias, q_p, kbt_p, vb_p, q_s, k_s, v_s,
                             pages_t(cache_k), pages_t(cache_v), batch=batch, seq=seq)

    y_p, tail_p, hlast_p = _rg_prompt(o_p, gate_p, xp, gate0p, shift1p, scale1p, g1, w_att_out, w_rg_in,
                                      gate1p, *rg_consts, batch=batch, seq=seq)
    y_s, tail_s, h_s = _rg_sample(o_s, gate_s, xs, gate0s, shift1s, scale1s, g1, w_att_out, w_rg_in,
                                  gate1s, jnp.swapaxes(state_conv[0], 0, 1), state_h[0], *rg_consts)

    def kv_rows(t):
        return jnp.transpose(t.reshape(1, batch, HEADS, HEAD_DIM, seq), (0, 1, 4, 2, 3))

    kv_s = (1, nb, 1, HEADS, HEAD_DIM)
    return (y_p.reshape(batch, seq, d), y_s.reshape(nb, 1, d),
            kv_rows(kt_p), kv_rows(vt_p), k_s.reshape(kv_s), v_s.reshape(kv_s),
            tail_p.reshape(1, batch, CONV_W - 1, d), hlast_p.reshape(1, batch, d),
            jnp.swapaxes(tail_s, 0, 1).reshape(1, nb, CONV_W - 1, d), h_s.reshape(1, nb, d))
```

```python
import functools

import jax
import jax.numpy as jnp
from jax import lax
from jax.experimental import pallas as pl
from jax.experimental.pallas import tpu as pltpu

F32 = jnp.float32
BF16 = jnp.bfloat16

D_MODEL = 1024
HEADS = 16
HEAD_DIM = 64
LANES = 128
SUBLANES = 8
HEADS_PER_LANE_TILE = LANES // HEAD_DIM
PAGE = 128
RG_BLOCKS = 4
RG_BLOCK_W = D_MODEL // RG_BLOCKS
CONV_W = 4
RG_C = 8.0
EPS = 1e-6
Q_SCALE = HEAD_DIM ** -0.5
LOG2_E = 1.4426950408889634
F32_TINY = 1.1754943508222875e-38

ROW_TILE = 512
ATT_TILE = 256
RG_TILE = 256
PAGES_PER_STEP = 8
CHUNKS_PER_STEP = 2
VMEM_LIMIT = 56 * 1024 * 1024

NT_DIMS = (((1,), (1,)), ((), ()))


def _dot(a, b):
    return jnp.dot(a, b, preferred_element_type=F32)


def _dot_nt(a, b):
    return lax.dot_general(a, b, NT_DIMS, preferred_element_type=F32)


def _sigmoid(x):
    return 0.5 * jnp.tanh(0.5 * x) + 0.5


def _silu(x):
    return x * _sigmoid(x)


def _neg_softplus(z):
    return -(jnp.maximum(z, 0.0) + jnp.log(1.0 + jnp.exp(-jnp.abs(z))))


def _split2(x):
    hi = x.astype(BF16)
    lo = (x - hi.astype(F32)).astype(BF16)
    return hi, lo


def _split3(x):
    hi = x.astype(BF16)
    r = x - hi.astype(F32)
    mid = r.astype(BF16)
    lo = (r - mid.astype(F32)).astype(BF16)
    return hi, mid, lo


def _dot3(x, m):
    hi, mid, lo = _split3(x)
    return _dot(hi, m) + _dot(mid, m) + _dot(lo, m)


def _rmsnorm(x, g):
    ms = jnp.mean(x * x, axis=-1, keepdims=True)
    return (x * lax.rsqrt(ms + EPS)) * g


def _params(semantics):
    return pltpu.CompilerParams(dimension_semantics=semantics, vmem_limit_bytes=VMEM_LIMIT)


def _resident(shape):
    zeros = (0,) * len(shape)
    return pl.BlockSpec(shape, lambda *_: zeros, pipeline_mode=pl.Buffered(1))


def _mod_kernel(c_ref, w_ref, b_ref, o_ref):
    c = c_ref[...]
    o_ref[...] = _dot(_silu(c).astype(BF16), w_ref[...].astype(BF16)) + b_ref[...]


def _modulation(c_all, mod_w, mod_b):
    depth, d, n = mod_w.shape
    rows = c_all.shape[0]
    tn = 1024
    return pl.pallas_call(
        _mod_kernel,
        out_shape=jax.ShapeDtypeStruct((depth, rows, n), F32),
        grid=(depth, n // tn),
        in_specs=[
            pl.BlockSpec((rows, d), lambda l, j: (0, 0)),
            pl.BlockSpec((None, d, tn), lambda l, j: (l, 0, j)),
            pl.BlockSpec((None, 1, tn), lambda l, j: (l, 0, j)),
        ],
        out_specs=pl.BlockSpec((None, rows, tn), lambda l, j: (l, 0, j)),
        compiler_params=_params(("arbitrary", "arbitrary")),
        name="modulation",
    )(c_all, mod_w, mod_b.reshape(depth, 1, n))


def _att_in_hidden(x_ref, shift_ref, scale_ref, g_ref):
    h = _rmsnorm(x_ref[...], g_ref[...]) * (1.0 + scale_ref[...]) + shift_ref[...]
    return h.astype(BF16)


def _att_in_prompt_kernel(x_ref, shift_ref, scale_ref, g_ref, w_ref, wkv_t_ref,
                          q_ref, kt_ref, vt_ref, gate_ref, kbt_ref, vb_ref):
    d = D_MODEL
    t = ATT_TILE
    h = _att_in_hidden(x_ref, shift_ref, scale_ref, g_ref)
    q_ref[...] = (_dot(h, w_ref[:, 0:d]) * Q_SCALE).astype(BF16)
    kt = _dot_nt(wkv_t_ref[0:d, :], h)
    kt_ref[...] = kt
    for j in range(kbt_ref.shape[0]):
        kbt_ref[j] = kt[:, j * t:(j + 1) * t].astype(BF16)
    vt_ref[...] = _dot_nt(wkv_t_ref[d:2 * d, :], h)
    vb_ref[...] = _dot(h, w_ref[:, 2 * d:3 * d]).astype(BF16)
    gate_ref[...] = _dot(h, w_ref[:, 3 * d:4 * d])


def _att_in_prompt(x, shift, scale, g, w, wkv_t, *, batch, seq):
    r, d = x.shape
    tm = ROW_TILE
    t = ATT_TILE
    per_b = seq // tm
    row = pl.BlockSpec((tm, d), lambda i: (i, 0))
    mod = pl.BlockSpec((None, 1, d), lambda i: (i // per_b, 0, 0))
    feat = pl.BlockSpec((None, d, tm), lambda i: (i // per_b, 0, i % per_b))
    feat_tiles = pl.BlockSpec((None, tm // t, d, t), lambda i: (i // per_b, i % per_b, 0, 0))
    return pl.pallas_call(
        _att_in_prompt_kernel,
        out_shape=(
            jax.ShapeDtypeStruct((r, d), BF16),
            jax.ShapeDtypeStruct((batch, d, seq), F32),
            jax.ShapeDtypeStruct((batch, d, seq), F32),
            jax.ShapeDtypeStruct((r, d), F32),
            jax.ShapeDtypeStruct((batch, seq // t, d, t), BF16),
            jax.ShapeDtypeStruct((r, d), BF16),
        ),
        grid=(r // tm,),
        in_specs=[row, mod, mod, _resident((1, d)), _resident((d, 4 * d)), _resident((2 * d, d))],
        out_specs=(row, feat, feat, row, feat_tiles, row),
        compiler_params=_params(("arbitrary",)),
        name="att_in_prompt",
    )(x, shift, scale, g, w, wkv_t)


def _att_in_sample_kernel(x_ref, shift_ref, scale_ref, g_ref, w_ref, q_ref, k_ref, v_ref, gate_ref):
    d = D_MODEL
    h = _att_in_hidden(x_ref, shift_ref, scale_ref, g_ref)
    q_ref[...] = (_dot(h, w_ref[:, 0:d]) * Q_SCALE).astype(BF16)
    k_ref[...] = _dot(h, w_ref[:, d:2 * d])
    v_ref[...] = _dot(h, w_ref[:, 2 * d:3 * d])
    gate_ref[...] = _dot(h, w_ref[:, 3 * d:4 * d])


def _att_in_sample(x, shift, scale, g, w):
    r, d = x.shape
    sds = jax.ShapeDtypeStruct((r, d), F32)
    return pl.pallas_call(
        _att_in_sample_kernel,
        out_shape=(jax.ShapeDtypeStruct((r, d), BF16), sds, sds, sds),
        compiler_params=pltpu.CompilerParams(vmem_limit_bytes=VMEM_LIMIT),
        name="att_in_sample",
    )(x, shift, scale, g, w)


MASKED_LOG = -1e30


def _sb_scores(nqs, biases, k, masks):
    t = ATT_TILE
    nz_all = _dot(jnp.concatenate(nqs, axis=0), k)
    log_betas, splits, totals = [], [], []
    for c in range(len(nqs)):
        nz = nz_all[c * t:(c + 1) * t] - biases[c]
        s = jnp.log(1.0 + jnp.exp2(jnp.abs(nz) * (-LOG2_E)))
        lk = jnp.minimum(nz, 0.0) - s
        lb = lk - nz
        if masks[c] is not None:
            lk = jnp.where(masks[c], lk, 0.0)
            lb = jnp.where(masks[c], lb, MASKED_LOG)
        log_betas.append(lb)
        splits.append(jnp.concatenate(_split2(lk), axis=1))
        totals.append(jnp.sum(lk, axis=-1, keepdims=True))
    return log_betas, splits, totals


def _sb_apply(scores, v, u2, carries, accs):
    t = ATT_TILE
    log_betas, splits, totals = scores
    n = len(log_betas)
    later_all = _dot(jnp.concatenate(splits, axis=0), u2)
    weights = [jnp.exp(log_betas[c] + later_all[c * t:(c + 1) * t] + carries[c]).astype(BF16)
               for c in range(n)]
    av_all = _dot(jnp.concatenate(weights, axis=0), v)
    accs = [accs[c] + av_all[c * t:(c + 1) * t] for c in range(n)]
    carries = [carries[c] + totals[c] for c in range(n)]
    return carries, accs


def _sb_prompt_step(bias_ref, q_ref, kt_ref, v_ref, u2_ref, o_ref, hp, i):
    t = ATT_TILE
    lane = lax.broadcasted_iota(jnp.int32, (1, LANES), 1)
    row = lax.broadcasted_iota(jnp.int32, (t, t), 0)
    col = lax.broadcasted_iota(jnp.int32, (t, t), 1)
    causal = col < row
    nq = -q_ref[...].astype(F32)
    u2 = u2_ref[...]

    def values(j):
        return v_ref[pl.ds(pl.multiple_of(j * t, t), t), :]

    nqs, biases = [], []
    for hh in range(HEADS_PER_LANE_TILE):
        in_head = (lane >= hh * HEAD_DIM) & (lane < (hh + 1) * HEAD_DIM)
        masked = jnp.where(in_head, nq, 0.0).astype(BF16)
        nqs += [masked[0:t], masked[t:2 * t]]
        biases += [bias_ref[hp * HEADS_PER_LANE_TILE + hh]] * 2
    n = len(nqs)
    upper = [c for c in range(n) if c % 2 == 1]
    pick = lambda xs: [xs[c] for c in upper]

    carries = [jnp.zeros((t, 1), F32)] * n
    accs = [jnp.zeros((t, LANES), F32)] * n
    first = _sb_scores(pick(nqs), pick(biases), kt_ref[2 * i + 1], [causal] * len(upper))
    second = _sb_scores(nqs, biases, kt_ref[2 * i], [None if c in upper else causal for c in range(n)])
    cu, au = _sb_apply(first, values(2 * i + 1), u2, pick(carries), pick(accs))
    for idx, c in enumerate(upper):
        carries[c], accs[c] = cu[idx], au[idx]
    carries, accs = _sb_apply(second, values(2 * i), u2, carries, accs)

    def body(it, st):
        j = 2 * (i - it) - 1
        first = _sb_scores(nqs, biases, kt_ref[j], [None] * n)
        second = _sb_scores(nqs, biases, kt_ref[j - 1], [None] * n)
        cs, as_ = _sb_apply(first, values(j), u2, list(st[0]), list(st[1]))
        cs, as_ = _sb_apply(second, values(j - 1), u2, cs, as_)
        return tuple(cs), tuple(as_)

    carries, accs = lax.fori_loop(0, i, body, (tuple(carries), tuple(accs)))
    for r in range(2):
        o_ref[r * t:(r + 1) * t, :] = jnp.where(lane < HEAD_DIM, accs[r], accs[2 + r])


def _sb_sample_chunk(k_refs, v_refs, bias_ref, u_ref, ones, qbd_ref, acc_ref, carry_ref, a_ref):
    g_n = len(k_refs)
    hd = HEAD_DIM
    qbd = qbd_ref[...]
    z = jnp.concatenate([_dot(qbd, k_refs[g][...].astype(BF16)) for g in range(g_n)], axis=0)
    z = z + bias_ref[...]
    lk = _neg_softplus(z)
    hi, mid, lo = _split3(lk)
    u = u_ref[...]
    later = _dot(hi, u) + _dot(mid, u) + _dot(lo, u)
    tot = _dot(hi, ones) + _dot(mid, ones) + _dot(lo, ones)
    ex = z + lk + later
    off = carry_ref[...]
    for g in reversed(range(g_n)):
        sl = slice(g * HEADS, (g + 1) * HEADS)
        a_ref[sl, :] = jnp.exp(ex[sl] + off)
        off = off + tot[sl]
    carry_ref[...] = off
    for h in range(HEADS):
        feat = slice(h * hd, (h + 1) * hd)
        acc = acc_ref[feat, :]
        for g in range(g_n):
            acc = acc + a_ref[g * HEADS + h:g * HEADS + h + 1, :] * v_refs[g][feat, :]
        acc_ref[feat, :] = acc


def _sb_sample_step(first, last, q_ref, kn_ref, vn_ref, bias_ref, bias_row_ref, u_ref, ones_ref, hsel_ref,
                    k_chunks, v_chunks, o_ref, qbd_ref, acc_ref, carry_ref, a_ref, *, beside):
    d = D_MODEL
    ones = ones_ref[...]

    def new_token():
        mask_new = jnp.zeros((SUBLANES, LANES), jnp.int32) < jnp.zeros((SUBLANES, LANES), jnp.int32)
        qk = q_ref[...].astype(F32) * kn_ref[...].astype(BF16).astype(F32)
        z_new = _dot3(jnp.broadcast_to(qk, (SUBLANES, d)), hsel_ref[...]) + bias_row_ref[...]
        lk_new = _neg_softplus(z_new)
        a_new = jnp.where(mask_new, jnp.exp(z_new + lk_new), 0.0)
        return a_new, jnp.where(mask_new, lk_new, 0.0)

    @pl.when(first)
    def _():
        head = lax.broadcasted_iota(jnp.int32, (HEADS, d), 0)
        feat = lax.broadcasted_iota(jnp.int32, (HEADS, d), 1)
        own = (feat >= head * HEAD_DIM) & (feat < (head + 1) * HEAD_DIM)
        qbd_ref[...] = jnp.where(own, jnp.broadcast_to(q_ref[...].astype(F32), (HEADS, d)), 0.0).astype(BF16)
        acc_ref[...] = jnp.zeros((d, PAGE), F32)
        _, lk_new = new_token()
        head_diag = (lax.broadcasted_iota(jnp.int32, (HEADS, LANES), 0)
                     == lax.broadcasted_iota(jnp.int32, (HEADS, LANES), 1))
        carry_ref[...] = _dot3(jnp.where(head_diag, jnp.broadcast_to(lk_new[0:1, :], (HEADS, LANES)), 0.0), ones)

    for k_refs, v_refs in zip(k_chunks, v_chunks):
        _sb_sample_chunk(k_refs, v_refs, bias_ref, u_ref, ones, qbd_ref, acc_ref, carry_ref, a_ref)
    beside()

    @pl.when(last)
    def _():
        hi_a, mid_a, lo_a = _split3(acc_ref[...])
        ones8 = jnp.ones((SUBLANES, PAGE), BF16)
        o = _dot_nt(ones8, hi_a) + _dot_nt(ones8, mid_a) + _dot_nt(ones8, lo_a)
        a_new, _ = new_token()
        o_new = _dot_nt(a_new.astype(BF16), hsel_ref[...]) * vn_ref[...]
        o_ref[...] = o[0:1, :] + o_new[0:1, :]


def _page_copy(cache_ref, buf_ref, sem_ref, page, half, k):
    return pltpu.make_async_copy(cache_ref.at[page], buf_ref.at[half, k], sem_ref.at[half])


def _sb_attention_kernel(step_pages_ref, bias_ref, q_ref, kt_ref, v_ref, u2_ref,
                         qs_ref, kn_ref, vn_ref, bias_col_ref, bias_row_ref, u_ref, ones_ref, hsel_ref,
                         cache_k_ref, cache_v_ref, o_ref, os_ref,
                         qbd_ref, acc_ref, carry_ref, a_ref, kbuf_ref, vbuf_ref, ksem_ref, vsem_ref, *, per_seq):
    g_n, n_chunks = PAGES_PER_STEP, CHUNKS_PER_STEP
    n = g_n * n_chunks
    b, hp, i = pl.program_id(0), pl.program_id(1), pl.program_id(2)
    n_steps = pl.num_programs(0) * pl.num_programs(1) * pl.num_programs(2)
    step = (b * pl.num_programs(1) + hp) * pl.num_programs(2) + i
    half = step & 1

    def fetch(s, into):
        for k in range(n):
            page = step_pages_ref[s * n + k]
            _page_copy(cache_k_ref, kbuf_ref, ksem_ref, page, into, k).start()
            _page_copy(cache_v_ref, vbuf_ref, vsem_ref, page, into, k).start()

    @pl.when(step == 0)
    def _():
        fetch(step, half)

    @pl.when(step + 1 < n_steps)
    def _():
        fetch(step + 1, 1 - half)

    for k in range(n):
        _page_copy(cache_k_ref, kbuf_ref, ksem_ref, 0, half, k).wait()
        _page_copy(cache_v_ref, vbuf_ref, vsem_ref, 0, half, k).wait()

    chunks = lambda buf: [[buf.at[half, c * g_n + g] for g in range(g_n)] for c in range(n_chunks)]
    part = step & (per_seq - 1)
    prompt = functools.partial(_sb_prompt_step, bias_ref, q_ref, kt_ref, v_ref, u2_ref, o_ref, hp, i)
    _sb_sample_step(part == 0, part == per_seq - 1, qs_ref, kn_ref, vn_ref, bias_col_ref, bias_row_ref,
                    u_ref, ones_ref, hsel_ref, chunks(kbuf_ref), chunks(vbuf_ref), os_ref,
                    qbd_ref, acc_ref, carry_ref, a_ref, beside=prompt)


def _sb_attention(page_table, bias, q, kbt, vb, q_s, k_new, v_new, cache_kt, cache_vt, *, batch, seq):
    d = q.shape[1]
    t = ATT_TILE
    nq = seq // t
    steps = nq // 2
    nb = q_s.shape[0]
    g_n, n_chunks = PAGES_PER_STEP, CHUNKS_PER_STEP
    nc = page_table.shape[1] // g_n
    per_seq = nc // n_chunks
    heads2 = d // LANES
    assert batch * heads2 * steps == nb * per_seq
    i32 = jnp.int32

    idx = jnp.arange(t, dtype=i32)
    u_t = (idx[:, None] > idx[None, :]).astype(BF16)
    u2 = jnp.concatenate([u_t, u_t], axis=0)
    key = jnp.arange(PAGE, dtype=i32)
    u = (key[:, None] > key[None, :]).astype(BF16)
    ones = jnp.ones((LANES, LANES), BF16)
    feat_head = jnp.arange(d, dtype=i32) // HEAD_DIM
    hsel = (feat_head[:, None] == jnp.arange(LANES, dtype=i32)[None, :]).astype(BF16)
    bias_col = jnp.broadcast_to(jnp.tile(bias, g_n)[:, None], (g_n * HEADS, PAGE))
    bias_row = jnp.pad(bias, (0, LANES - HEADS)).reshape(1, LANES)

    n = g_n * n_chunks
    n_steps = batch * heads2 * steps
    s_idx = jnp.arange(n_steps, dtype=i32)
    slot = jnp.arange(n, dtype=i32)
    chunk = (s_idx % per_seq)[:, None] * n_chunks + (slot // g_n)[None, :]
    step_pages = page_table[(s_idx // per_seq)[:, None], (nc - 1 - chunk) * g_n + (slot % g_n)[None, :]]
    step_pages = step_pages.reshape(n_steps * n).astype(i32)
    log2_per_seq = per_seq.bit_length() - 1
    assert per_seq == 1 << log2_per_seq

    def step_of(b, h, i):
        return (b * heads2 + h) * steps + i

    const2 = lambda b, h, i, sp, _: (0, 0)
    row3 = pl.BlockSpec(
        (None, 1, d), lambda b, h, i, sp, _: (lax.shift_right_logical(step_of(b, h, i), log2_per_seq), 0, 0))
    pool = pl.BlockSpec(memory_space=pl.ANY)
    grid_spec = pltpu.PrefetchScalarGridSpec(
        num_scalar_prefetch=2,
        grid=(batch, heads2, steps),
        in_specs=[pl.BlockSpec((2 * t, LANES), lambda b, h, i, pt, _: (b * steps + i, h)),
                  pl.BlockSpec((None, nq, LANES, t), lambda b, h, i, pt, _: (b, 0, h, 0)),
                  pl.BlockSpec((seq, LANES), lambda b, h, i, pt, _: (b, h)),
                  pl.BlockSpec((2 * t, t), const2),
                  row3, row3, row3,
                  pl.BlockSpec((g_n * HEADS, PAGE), const2),
                  pl.BlockSpec((1, LANES), const2),
                  pl.BlockSpec((PAGE, PAGE), const2),
                  pl.BlockSpec((LANES, LANES), const2),
                  pl.BlockSpec((d, LANES), const2),
                  pool, pool],
        out_specs=(pl.BlockSpec((2 * t, LANES), lambda b, h, i, pt, _: (b * steps + i, h)), row3),
        scratch_shapes=[pltpu.VMEM((HEADS, d), BF16),
                        pltpu.VMEM((d, PAGE), F32),
                        pltpu.VMEM((HEADS, PAGE), F32),
                        pltpu.VMEM((g_n * HEADS, PAGE), F32),
                        pltpu.VMEM((2, n, d, PAGE), F32),
                        pltpu.VMEM((2, n, d, PAGE), F32),
                        pltpu.SemaphoreType.DMA((2,)),
                        pltpu.SemaphoreType.DMA((2,))],
    )
    o_p, o_s = pl.pallas_call(
        functools.partial(_sb_attention_kernel, per_seq=per_seq),
        out_shape=(jax.ShapeDtypeStruct((batch * seq, d), F32), jax.ShapeDtypeStruct((nb, 1, d), F32)),
        grid_spec=grid_spec,
        compiler_params=_params(("arbitrary", "arbitrary", "arbitrary")),
        name="sb_attention",
    )(step_pages, bias, q, kbt, vb, u2,
      q_s.reshape(nb, 1, d), k_new.reshape(nb, 1, d), v_new.reshape(nb, 1, d),
      bias_col, bias_row, u, ones, hsel, cache_kt, cache_vt)
    return o_p, o_s.reshape(nb, d)


def _att_out_rg_in(o, gate, x, mgate, shift, scale, g, wout_ref, win_ref):
    d = D_MODEL
    y = _dot((o * _silu(gate)).astype(BF16), wout_ref[...])
    x1 = x + mgate * y
    h = (_rmsnorm(x1, g) * (1.0 + scale) + shift).astype(BF16)
    return x1, _dot(h, win_ref[:, 0:d]), _dot(h, win_ref[:, d:2 * d])


def _rg_gates(xc, wa_ref, ba_ref, wx_ref, bx_ref, lam_ref):
    xcb = xc.astype(BF16)
    w = RG_BLOCK_W
    ra = jnp.concatenate([_dot(xcb[:, n * w:(n + 1) * w], wa_ref[n]) for n in range(RG_BLOCKS)], axis=-1)
    rx = jnp.concatenate([_dot(xcb[:, n * w:(n + 1) * w], wx_ref[n]) for n in range(RG_BLOCKS)], axis=-1)
    r = _sigmoid(ra + ba_ref[...])
    ig = _sigmoid(rx + bx_ref[...])
    lam = lam_ref[...]
    log_sig_lam = -(jnp.maximum(-lam, 0.0) + jnp.log1p(jnp.exp(-jnp.abs(lam))))
    log_a = RG_C * r * log_sig_lam
    a = jnp.exp(log_a)
    gap = -jnp.tanh(log_a) * (a * a + 1.0)
    u = (gap * lax.rsqrt(jnp.maximum(gap, F32_TINY))) * (ig * xc)
    return a, u


def _rg_finish(y, gbr, x1, mgate, wout_ref, fg_ref):
    out = _dot((y * _silu(gbr)).astype(BF16), wout_ref[...])
    return _rmsnorm(x1 + mgate * out, fg_ref[...])


def _rg_prompt_kernel(o_ref, gate_ref, xin_ref, mgate0_ref, shift_ref, scale_ref, g_ref, wao_ref, win_ref,
                      mgate_ref, cw_ref, cb_ref, wa_ref, ba_ref, wx_ref, bx_ref, lam_ref, wout_ref, fg_ref,
                      y_ref, tail_ref, hlast_ref, hist_s, a_s, u_s, h_s):
    tt = RG_TILE
    d = D_MODEL
    ti = pl.program_id(1)
    pad = SUBLANES

    @pl.when(ti == 0)
    def _():
        hist_s[...] = jnp.zeros((pad, d), F32)
        h_s[...] = jnp.zeros((SUBLANES, d), F32)

    x1, x, gbr = _att_out_rg_in(o_ref[...], gate_ref[...], xin_ref[...], mgate0_ref[...], shift_ref[...],
                                scale_ref[...], g_ref[...], wao_ref, win_ref)
    full = jnp.concatenate([hist_s[...], x], axis=0)
    xc = cb_ref[...] + cw_ref[3:4, :] * x
    for k in range(1, CONV_W):
        xc = xc + cw_ref[CONV_W - 1 - k:CONV_W - k, :] * pltpu.roll(full, k, axis=0)[pad:pad + tt, :]
    hist_s[...] = x[tt - pad:tt, :]

    a, u = _rg_gates(xc, wa_ref, ba_ref, wx_ref, bx_ref, lam_ref)
    a_s[...] = a
    u_s[...] = u
    sub = lax.broadcasted_iota(jnp.int32, (SUBLANES, d), 0)

    def group(i, h):
        r0 = pl.multiple_of(i * SUBLANES, SUBLANES)
        ag = a_s[pl.ds(r0, SUBLANES), :]
        ug = u_s[pl.ds(r0, SUBLANES), :]
        for s in (1, 2, 4):
            a_prev = jnp.where(sub >= s, pltpu.roll(ag, s, axis=0), 1.0)
            u_prev = jnp.where(sub >= s, pltpu.roll(ug, s, axis=0), 0.0)
            ug = ag * u_prev + ug
            ag = ag * a_prev
        hg = ag * h + ug
        u_s[pl.ds(r0, SUBLANES), :] = hg
        return jnp.broadcast_to(hg[SUBLANES - 1:SUBLANES, :], (SUBLANES, d))

    h = lax.fori_loop(0, tt // SUBLANES, group, h_s[...], unroll=4)
    h_s[...] = h

    y_ref[...] = _rg_finish(u_s[...], gbr, x1, mgate_ref[...], wout_ref, fg_ref)

    @pl.when(ti == pl.num_programs(1) - 1)
    def _():
        tail_ref[...] = x[tt - (CONV_W - 1):tt, :]
        hlast_ref[...] = h[0:1, :]


def _rg_prompt(o, gate, x, mgate0, shift, scale, g, w_att_out, w_rg_in,
               mgate, cw, cb, wa, ba, wx, bx, lam, wout, fg, *, batch, seq):
    d = D_MODEL
    tt = RG_TILE
    nt = seq // tt
    row = pl.BlockSpec((tt, d), lambda b, t: (b * nt + t, 0))
    per_seq = pl.BlockSpec((None, 1, d), lambda b, t: (b, 0, 0))
    vec = _resident((1, d))
    gates_w = _resident((RG_BLOCKS, RG_BLOCK_W, RG_BLOCK_W))
    return pl.pallas_call(
        _rg_prompt_kernel,
        out_shape=(jax.ShapeDtypeStruct((batch * seq, d), F32),
                   jax.ShapeDtypeStruct((batch, CONV_W - 1, d), F32),
                   jax.ShapeDtypeStruct((batch, 1, d), F32)),
        grid=(batch, nt),
        in_specs=[row, row, row, per_seq, per_seq, per_seq, vec, _resident((d, d)), _resident((d, 2 * d)),
                  per_seq,
                  _resident((CONV_W, d)),
                  vec, gates_w, vec, gates_w, vec, vec,
                  _resident((d, d)),
                  vec],
        out_specs=(row,
                   pl.BlockSpec((None, CONV_W - 1, d), lambda b, t: (b, 0, 0)),
                   per_seq),
        scratch_shapes=[pltpu.VMEM((SUBLANES, d), F32),
                        pltpu.VMEM((tt, d), F32),
                        pltpu.VMEM((tt, d), F32),
                        pltpu.VMEM((SUBLANES, d), F32)],
        compiler_params=_params(("arbitrary", "arbitrary")),
        name="rg_prompt",
    )(o, gate, x, mgate0, shift, scale, g, w_att_out, w_rg_in, mgate, cw, cb, wa, ba, wx, bx, lam, wout, fg)


def _rg_sample_kernel(o_ref, gate_ref, xin_ref, mgate0_ref, shift_ref, scale_ref, g_ref, wao_ref, win_ref,
                      mgate_ref, sconv_ref, sh_ref, cw_ref, cb_ref,
                      wa_ref, ba_ref, wx_ref, bx_ref, lam_ref, wout_ref, fg_ref,
                      y_ref, tail_ref, h_ref):
    x1, x, gbr = _att_out_rg_in(o_ref[...], gate_ref[...], xin_ref[...], mgate0_ref[...], shift_ref[...],
                                scale_ref[...], g_ref[...], wao_ref, win_ref)
    xc = cb_ref[...] + cw_ref[0:1, :] * sconv_ref[0]
    xc = xc + cw_ref[1:2, :] * sconv_ref[1]
    xc = xc + cw_ref[2:3, :] * sconv_ref[2]
    xc = xc + cw_ref[3:4, :] * x
    a, u = _rg_gates(xc, wa_ref, ba_ref, wx_ref, bx_ref, lam_ref)
    h = a * sh_ref[...] + u
    h_ref[...] = h
    tail_ref[0] = sconv_ref[1]
    tail_ref[1] = sconv_ref[2]
    tail_ref[2] = x
    y_ref[...] = _rg_finish(h, gbr, x1, mgate_ref[...], wout_ref, fg_ref)


def _rg_sample(o, gate, x, mgate0, shift, scale, g, w_att_out, w_rg_in,
               mgate, sconv, sh, cw, cb, wa, ba, wx, bx, lam, wout, fg):
    nb, d = x.shape
    sds = jax.ShapeDtypeStruct((nb, d), F32)
    return pl.pallas_call(
        _rg_sample_kernel,
        out_shape=(sds, jax.ShapeDtypeStruct((CONV_W - 1, nb, d), F32), sds),
        compiler_params=pltpu.CompilerParams(vmem_limit_bytes=VMEM_LIMIT),
        name="rg_sample",
    )(o, gate, x, mgate0, shift, scale, g, w_att_out, w_rg_in, mgate, sconv, sh,
      cw, cb, wa, ba, wx, bx, lam, wout, fg)


def kernel(x_prompt, x_sample, c_prompt, c_sample, cache_k, cache_v, page_table, state_conv, state_h,
           norm_g, mod_w, mod_b, att_w_in, att_bias, att_w_out, rg_w_in, rg_conv_w, rg_conv_b,
           rg_w_a, rg_b_a, rg_w_x, rg_b_x, rg_lambda, rg_w_out, final_g):
    batch, seq, d = x_prompt.shape
    nb, dec_seq, _ = x_sample.shape
    assert d == D_MODEL and dec_seq == 1 and mod_w.shape[0] == 2

    mod = _modulation(jnp.concatenate([c_prompt, c_sample], axis=0), mod_w, mod_b)

    def mod_parts(layer, lo, hi, shape):
        m = mod[layer, lo:hi]
        return tuple(m[:, i * d:(i + 1) * d].reshape(shape) for i in range(3))

    shift0p, scale0p, gate0p = mod_parts(0, 0, batch, (batch, 1, d))
    shift1p, scale1p, gate1p = mod_parts(1, 0, batch, (batch, 1, d))
    shift0s, scale0s, gate0s = mod_parts(0, batch, batch + nb, (nb, d))
    shift1s, scale1s, gate1s = mod_parts(1, batch, batch + nb, (nb, d))

    g0, g1 = norm_g[0].reshape(1, d), norm_g[1].reshape(1, d)
    w_att_in = att_w_in[0].astype(BF16)
    w_kv_t = jnp.swapaxes(att_w_in[0][:, d:3 * d], 0, 1).astype(BF16)
    w_att_out = att_w_out[0].astype(BF16)
    w_rg_in = rg_w_in[0].astype(BF16)
    w_rg_out = rg_w_out[0].astype(BF16)
    wa, wx = rg_w_a[0].astype(BF16), rg_w_x[0].astype(BF16)
    vec = lambda v: v.reshape(1, d)
    rg_consts = (rg_conv_w[0], vec(rg_conv_b[0]), wa, vec(rg_b_a[0]), wx, vec(rg_b_x[0]),
                 vec(rg_lambda[0]), w_rg_out, vec(final_g))
    bias = att_bias[0]

    xp = x_prompt.reshape(batch * seq, d)
    xs = x_sample.reshape(nb, d)
    q_p, kt_p, vt_p, gate_p, kbt_p, vb_p = _att_in_prompt(xp, shift0p, scale0p, g0, w_att_in, w_kv_t,
                                                          batch=batch, seq=seq)
    q_s, k_s, v_s, gate_s = _att_in_sample(xs, shift0s, scale0s, g0, w_att_in)
    pages_t = lambda cache: jnp.transpose(cache, (0, 1, 3, 4, 2)).reshape(-1, d, PAGE)
    o_p, o_s = _sb_attention(page_table, bias, q_p, kbt_p, vb_p, q_s, k_s, v_s,
                             pages_t(cache_k), pages_t(cache_v), batch=batch, seq=seq)

    y_p, tail_p, hlast_p = _rg_prompt(o_p, gate_p, xp, gate0p, shift1p, scale1p, g1, w_att_out, w_rg_in,
                                      gate1p, *rg_consts, batch=batch, seq=seq)
    y_s, tail_s, h_s = _rg_sample(o_s, gate_s, xs, gate0s, shift1s, scale1s, g1, w_att_out, w_rg_in,
                                  gate1s, jnp.swapaxes(state_conv[0], 0, 1), state_h[0], *rg_consts)

    def kv_rows(t):
        return jnp.transpose(t.reshape(1, batch, HEADS, HEAD_DIM, seq), (0, 1, 4, 2, 3))

    kv_s = (1, nb, 1, HEADS, HEAD_DIM)
    return (y_p.reshape(batch, seq, d), y_s.reshape(nb, 1, d),
            kv_rows(kt_p), kv_rows(vt_p), k_s.reshape(kv_s), v_s.reshape(kv_s),
            tail_p.reshape(1, batch, CONV_W - 1, d), hlast_p.reshape(1, batch, d),
            jnp.swapaxes(tail_s, 0, 1).reshape(1, nb, CONV_W - 1, d), h_s.reshape(1, nb, d))
```

```python
import functools

import jax
import jax.numpy as jnp
from jax import lax
from jax.experimental import pallas as pl
from jax.experimental.pallas import tpu as pltpu

F32 = jnp.float32
BF16 = jnp.bfloat16

D_MODEL = 1024
HEADS = 16
HEAD_DIM = 64
LANES = 128
SUBLANES = 8
HEADS_PER_LANE_TILE = LANES // HEAD_DIM
PAGE = 128
RG_BLOCKS = 4
RG_BLOCK_W = D_MODEL // RG_BLOCKS
CONV_W = 4
RG_C = 8.0
EPS = 1e-6
Q_SCALE = HEAD_DIM ** -0.5
LOG2_E = 1.4426950408889634
F32_TINY = 1.1754943508222875e-38

ROW_TILE = 512
ATT_TILE = 256
RG_TILE = 256
PAGES_PER_STEP = 8
CHUNKS_PER_STEP = 2
VMEM_LIMIT = 56 * 1024 * 1024

NT_DIMS = (((1,), (1,)), ((), ()))


def _dot(a, b):
    return jnp.dot(a, b, preferred_element_type=F32)


def _dot_nt(a, b):
    return lax.dot_general(a, b, NT_DIMS, preferred_element_type=F32)


def _sigmoid(x):
    return 0.5 * jnp.tanh(0.5 * x) + 0.5


def _silu(x):
    return x * _sigmoid(x)


def _neg_softplus(z):
    return -(jnp.maximum(z, 0.0) + jnp.log(1.0 + jnp.exp(-jnp.abs(z))))


def _split3(x):
    hi = x.astype(BF16)
    r = x - hi.astype(F32)
    mid = r.astype(BF16)
    lo = (r - mid.astype(F32)).astype(BF16)
    return hi, mid, lo


def _dot3(x, m):
    hi, mid, lo = _split3(x)
    return _dot(hi, m) + _dot(mid, m) + _dot(lo, m)


def _rmsnorm(x, g):
    ms = jnp.mean(x * x, axis=-1, keepdims=True)
    return (x * lax.rsqrt(ms + EPS)) * g


def _params(semantics):
    return pltpu.CompilerParams(dimension_semantics=semantics, vmem_limit_bytes=VMEM_LIMIT)


def _resident(shape):
    zeros = (0,) * len(shape)
    return pl.BlockSpec(shape, lambda *_: zeros, pipeline_mode=pl.Buffered(1))


def _mod_kernel(c_ref, w_ref, b_ref, o_ref):
    c = c_ref[...]
    o_ref[...] = _dot(_silu(c).astype(BF16), w_ref[...].astype(BF16)) + b_ref[...]


def _modulation(c_all, mod_w, mod_b):
    depth, d, n = mod_w.shape
    rows = c_all.shape[0]
    tn = 1024
    return pl.pallas_call(
        _mod_kernel,
        out_shape=jax.ShapeDtypeStruct((depth, rows, n), F32),
        grid=(depth, n // tn),
        in_specs=[
            pl.BlockSpec((rows, d), lambda l, j: (0, 0)),
            pl.BlockSpec((None, d, tn), lambda l, j: (l, 0, j)),
            pl.BlockSpec((None, 1, tn), lambda l, j: (l, 0, j)),
        ],
        out_specs=pl.BlockSpec((None, rows, tn), lambda l, j: (l, 0, j)),
        compiler_params=_params(("arbitrary", "arbitrary")),
        name="modulation",
    )(c_all, mod_w, mod_b.reshape(depth, 1, n))


def _att_in_hidden(x_ref, shift_ref, scale_ref, g_ref):
    h = _rmsnorm(x_ref[...], g_ref[...]) * (1.0 + scale_ref[...]) + shift_ref[...]
    return h.astype(BF16)


def _att_in_prompt_kernel(x_ref, shift_ref, scale_ref, g_ref, w_ref, wkv_t_ref,
                          q_ref, kt_ref, vt_ref, gate_ref, kbt_ref, vb_ref):
    d = D_MODEL
    t = ATT_TILE
    h = _att_in_hidden(x_ref, shift_ref, scale_ref, g_ref)
    q_ref[...] = (_dot(h, w_ref[:, 0:d]) * Q_SCALE).astype(BF16)
    kt = _dot_nt(wkv_t_ref[0:d, :], h)
    kt_ref[...] = kt
    for j in range(kbt_ref.shape[0]):
        kbt_ref[j] = kt[:, j * t:(j + 1) * t].astype(BF16)
    vt_ref[...] = _dot_nt(wkv_t_ref[d:2 * d, :], h)
    vb_ref[...] = _dot(h, w_ref[:, 2 * d:3 * d]).astype(BF16)
    gate_ref[...] = _dot(h, w_ref[:, 3 * d:4 * d])


def _att_in_prompt(x, shift, scale, g, w, wkv_t, *, batch, seq):
    r, d = x.shape
    tm = ROW_TILE
    t = ATT_TILE
    per_b = seq // tm
    row = pl.BlockSpec((tm, d), lambda i: (i, 0))
    mod = pl.BlockSpec((None, 1, d), lambda i: (i // per_b, 0, 0))
    feat = pl.BlockSpec((None, d, tm), lambda i: (i // per_b, 0, i % per_b))
    feat_tiles = pl.BlockSpec((None, tm // t, d, t), lambda i: (i // per_b, i % per_b, 0, 0))
    return pl.pallas_call(
        _att_in_prompt_kernel,
        out_shape=(
            jax.ShapeDtypeStruct((r, d), BF16),
            jax.ShapeDtypeStruct((batch, d, seq), F32),
            jax.ShapeDtypeStruct((batch, d, seq), F32),
            jax.ShapeDtypeStruct((r, d), F32),
            jax.ShapeDtypeStruct((batch, seq // t, d, t), BF16),
            jax.ShapeDtypeStruct((r, d), BF16),
        ),
        grid=(r // tm,),
        in_specs=[row, mod, mod, _resident((1, d)), _resident((d, 4 * d)), _resident((2 * d, d))],
        out_specs=(row, feat, feat, row, feat_tiles, row),
        compiler_params=_params(("arbitrary",)),
        name="att_in_prompt",
    )(x, shift, scale, g, w, wkv_t)


def _att_in_sample_kernel(x_ref, shift_ref, scale_ref, g_ref, w_ref, q_ref, k_ref, v_ref, gate_ref):
    d = D_MODEL
    h = _att_in_hidden(x_ref, shift_ref, scale_ref, g_ref)
    q_ref[...] = (_dot(h, w_ref[:, 0:d]) * Q_SCALE).astype(BF16)
    k_ref[...] = _dot(h, w_ref[:, d:2 * d])
    v_ref[...] = _dot(h, w_ref[:, 2 * d:3 * d])
    gate_ref[...] = _dot(h, w_ref[:, 3 * d:4 * d])


def _att_in_sample(x, shift, scale, g, w):
    r, d = x.shape
    sds = jax.ShapeDtypeStruct((r, d), F32)
    return pl.pallas_call(
        _att_in_sample_kernel,
        out_shape=(jax.ShapeDtypeStruct((r, d), BF16), sds, sds, sds),
        compiler_params=pltpu.CompilerParams(vmem_limit_bytes=VMEM_LIMIT),
        name="att_in_sample",
    )(x, shift, scale, g, w)


MASKED_LOG = -1e30


def _sb_scores(nqs, biases, k, masks):
    t = ATT_TILE
    nz_all = _dot(jnp.concatenate(nqs, axis=0), k)
    log_betas, log_keeps, totals = [], [], []
    for c in range(len(nqs)):
        nz = nz_all[c * t:(c + 1) * t] - biases[c]
        s = jnp.log(1.0 + jnp.exp2(jnp.abs(nz) * (-LOG2_E)))
        lk = jnp.minimum(nz, 0.0) - s
        lb = lk - nz
        if masks[c] is not None:
            lk = jnp.where(masks[c], lk, 0.0)
            lb = jnp.where(masks[c], lb, MASKED_LOG)
        log_betas.append(lb)
        log_keeps.append(lk)
        totals.append(jnp.sum(lk, axis=-1, keepdims=True))
    return log_betas, log_keeps, totals


def _sb_apply(scores, v, u, carries, accs):
    t = ATT_TILE
    log_betas, log_keeps, totals = scores
    n = len(log_betas)
    later_all = _dot(jnp.concatenate(log_keeps, axis=0), u)
    weights = [jnp.exp(log_betas[c] + later_all[c * t:(c + 1) * t] + carries[c]).astype(BF16)
               for c in range(n)]
    av_all = _dot(jnp.concatenate(weights, axis=0), v)
    accs = [accs[c] + av_all[c * t:(c + 1) * t] for c in range(n)]
    carries = [carries[c] + totals[c] for c in range(n)]
    return carries, accs


def _sb_prompt_step(bias_ref, q_ref, kt_ref, v_ref, tri_ref, o_ref, hp, i):
    t = ATT_TILE
    lane = lax.broadcasted_iota(jnp.int32, (1, LANES), 1)
    row = lax.broadcasted_iota(jnp.int32, (t, t), 0)
    col = lax.broadcasted_iota(jnp.int32, (t, t), 1)
    causal = col < row
    nq = -q_ref[...].astype(F32)
    tri = tri_ref[...].astype(F32)

    def values(j):
        return v_ref[pl.ds(pl.multiple_of(j * t, t), t), :]

    nqs, biases = [], []
    for hh in range(HEADS_PER_LANE_TILE):
        in_head = (lane >= hh * HEAD_DIM) & (lane < (hh + 1) * HEAD_DIM)
        masked = jnp.where(in_head, nq, 0.0).astype(BF16)
        nqs += [masked[0:t], masked[t:2 * t]]
        biases += [bias_ref[hp * HEADS_PER_LANE_TILE + hh]] * 2
    n = len(nqs)
    upper = [c for c in range(n) if c % 2 == 1]
    pick = lambda xs: [xs[c] for c in upper]

    carries = [jnp.zeros((t, 1), F32)] * n
    accs = [jnp.zeros((t, LANES), F32)] * n
    first = _sb_scores(pick(nqs), pick(biases), kt_ref[2 * i + 1], [causal] * len(upper))
    second = _sb_scores(nqs, biases, kt_ref[2 * i], [None if c in upper else causal for c in range(n)])
    cu, au = _sb_apply(first, values(2 * i + 1), tri, pick(carries), pick(accs))
    for idx, c in enumerate(upper):
        carries[c], accs[c] = cu[idx], au[idx]
    carries, accs = _sb_apply(second, values(2 * i), tri, carries, accs)

    def body(it, st):
        j = 2 * (i - it) - 1
        first = _sb_scores(nqs, biases, kt_ref[j], [None] * n)
        second = _sb_scores(nqs, biases, kt_ref[j - 1], [None] * n)
        cs, as_ = _sb_apply(first, values(j), tri, list(st[0]), list(st[1]))
        cs, as_ = _sb_apply(second, values(j - 1), tri, cs, as_)
        return tuple(cs), tuple(as_)

    carries, accs = lax.fori_loop(0, i, body, (tuple(carries), tuple(accs)))
    for r in range(2):
        o_ref[r * t:(r + 1) * t, :] = jnp.where(lane < HEAD_DIM, accs[r], accs[2 + r])


def _sb_sample_chunk(k_refs, v_refs, bias_ref, u_ref, ones, qbd_ref, acc_ref, carry_ref, a_ref):
    g_n = len(k_refs)
    hd = HEAD_DIM
    qbd = qbd_ref[...]
    z = jnp.concatenate([_dot(qbd, k_refs[g][...].astype(BF16)) for g in range(g_n)], axis=0)
    z = z + bias_ref[...]
    lk = _neg_softplus(z)
    hi, mid, lo = _split3(lk)
    u = u_ref[...]
    later = _dot(hi, u) + _dot(mid, u) + _dot(lo, u)
    tot = _dot(hi, ones) + _dot(mid, ones) + _dot(lo, ones)
    ex = z + lk + later
    off = carry_ref[...]
    for g in reversed(range(g_n)):
        sl = slice(g * HEADS, (g + 1) * HEADS)
        a_ref[sl, :] = jnp.exp(ex[sl] + off)
        off = off + tot[sl]
    carry_ref[...] = off
    for h in range(HEADS):
        feat = slice(h * hd, (h + 1) * hd)
        acc = acc_ref[feat, :]
        for g in range(g_n):
            acc = acc + a_ref[g * HEADS + h:g * HEADS + h + 1, :] * v_refs[g][feat, :]
        acc_ref[feat, :] = acc


def _sb_sample_step(first, last, q_ref, kn_ref, vn_ref, bias_ref, bias_row_ref, u_ref, ones_ref, hsel_ref,
                    k_chunks, v_chunks, o_ref, qbd_ref, acc_ref, carry_ref, a_ref, *, beside):
    d = D_MODEL
    ones = ones_ref[...]

    def new_token():
        mask_new = jnp.zeros((SUBLANES, LANES), jnp.int32) < jnp.zeros((SUBLANES, LANES), jnp.int32)
        qk = q_ref[...].astype(F32) * kn_ref[...].astype(BF16).astype(F32)
        z_new = _dot3(jnp.broadcast_to(qk, (SUBLANES, d)), hsel_ref[...]) + bias_row_ref[...]
        lk_new = _neg_softplus(z_new)
        a_new = jnp.where(mask_new, jnp.exp(z_new + lk_new), 0.0)
        return a_new, jnp.where(mask_new, lk_new, 0.0)

    @pl.when(first)
    def _():
        head = lax.broadcasted_iota(jnp.int32, (HEADS, d), 0)
        feat = lax.broadcasted_iota(jnp.int32, (HEADS, d), 1)
        own = (feat >= head * HEAD_DIM) & (feat < (head + 1) * HEAD_DIM)
        qbd_ref[...] = jnp.where(own, jnp.broadcast_to(q_ref[...].astype(F32), (HEADS, d)), 0.0).astype(BF16)
        acc_ref[...] = jnp.zeros((d, PAGE), F32)
        _, lk_new = new_token()
        head_diag = (lax.broadcasted_iota(jnp.int32, (HEADS, LANES), 0)
                     == lax.broadcasted_iota(jnp.int32, (HEADS, LANES), 1))
        carry_ref[...] = _dot3(jnp.where(head_diag, jnp.broadcast_to(lk_new[0:1, :], (HEADS, LANES)), 0.0), ones)

    for k_refs, v_refs in zip(k_chunks, v_chunks):
        _sb_sample_chunk(k_refs, v_refs, bias_ref, u_ref, ones, qbd_ref, acc_ref, carry_ref, a_ref)
    beside()

    @pl.when(last)
    def _():
        hi_a, mid_a, lo_a = _split3(acc_ref[...])
        ones8 = jnp.ones((SUBLANES, PAGE), BF16)
        o = _dot_nt(ones8, hi_a) + _dot_nt(ones8, mid_a) + _dot_nt(ones8, lo_a)
        a_new, _ = new_token()
        o_new = _dot_nt(a_new.astype(BF16), hsel_ref[...]) * vn_ref[...]
        o_ref[...] = o[0:1, :] + o_new[0:1, :]


def _page_copy(cache_ref, buf_ref, sem_ref, page, half, k):
    return pltpu.make_async_copy(cache_ref.at[page], buf_ref.at[half, k], sem_ref.at[half])


def _sb_attention_kernel(step_pages_ref, bias_ref, q_ref, kt_ref, v_ref, tri_ref,
                         qs_ref, kn_ref, vn_ref, bias_col_ref, bias_row_ref, u_ref, ones_ref, hsel_ref,
                         cache_k_ref, cache_v_ref, o_ref, os_ref,
                         qbd_ref, acc_ref, carry_ref, a_ref, kbuf_ref, vbuf_ref, ksem_ref, vsem_ref, *, per_seq):
    g_n, n_chunks = PAGES_PER_STEP, CHUNKS_PER_STEP
    n = g_n * n_chunks
    b, hp, i = pl.program_id(0), pl.program_id(1), pl.program_id(2)
    n_steps = pl.num_programs(0) * pl.num_programs(1) * pl.num_programs(2)
    step = (b * pl.num_programs(1) + hp) * pl.num_programs(2) + i
    half = step & 1

    def fetch(s, into):
        for k in range(n):
            page = step_pages_ref[s * n + k]
            _page_copy(cache_k_ref, kbuf_ref, ksem_ref, page, into, k).start()
            _page_copy(cache_v_ref, vbuf_ref, vsem_ref, page, into, k).start()

    @pl.when(step == 0)
    def _():
        fetch(step, half)

    @pl.when(step + 1 < n_steps)
    def _():
        fetch(step + 1, 1 - half)

    for k in range(n):
        _page_copy(cache_k_ref, kbuf_ref, ksem_ref, 0, half, k).wait()
        _page_copy(cache_v_ref, vbuf_ref, vsem_ref, 0, half, k).wait()

    chunks = lambda buf: [[buf.at[half, c * g_n + g] for g in range(g_n)] for c in range(n_chunks)]
    part = step & (per_seq - 1)
    prompt = functools.partial(_sb_prompt_step, bias_ref, q_ref, kt_ref, v_ref, tri_ref, o_ref, hp, i)
    _sb_sample_step(part == 0, part == per_seq - 1, qs_ref, kn_ref, vn_ref, bias_col_ref, bias_row_ref,
                    u_ref, ones_ref, hsel_ref, chunks(kbuf_ref), chunks(vbuf_ref), os_ref,
                    qbd_ref, acc_ref, carry_ref, a_ref, beside=prompt)


def _sb_attention(page_table, bias, q, kbt, vb, q_s, k_new, v_new, cache_kt, cache_vt, *, batch, seq):
    d = q.shape[1]
    t = ATT_TILE
    nq = seq // t
    steps = nq // 2
    nb = q_s.shape[0]
    g_n, n_chunks = PAGES_PER_STEP, CHUNKS_PER_STEP
    nc = page_table.shape[1] // g_n
    per_seq = nc // n_chunks
    heads2 = d // LANES
    assert batch * heads2 * steps == nb * per_seq
    i32 = jnp.int32

    idx = jnp.arange(t, dtype=i32)
    tri = (idx[:, None] > idx[None, :]).astype(BF16)
    key = jnp.arange(PAGE, dtype=i32)
    u = (key[:, None] > key[None, :]).astype(BF16)
    ones = jnp.ones((LANES, LANES), BF16)
    feat_head = jnp.arange(d, dtype=i32) // HEAD_DIM
    hsel = (feat_head[:, None] == jnp.arange(LANES, dtype=i32)[None, :]).astype(BF16)
    bias_col = jnp.broadcast_to(jnp.tile(bias, g_n)[:, None], (g_n * HEADS, PAGE))
    bias_row = jnp.pad(bias, (0, LANES - HEADS)).reshape(1, LANES)

    n = g_n * n_chunks
    n_steps = batch * heads2 * steps
    s_idx = jnp.arange(n_steps, dtype=i32)
    slot = jnp.arange(n, dtype=i32)
    chunk = (s_idx % per_seq)[:, None] * n_chunks + (slot // g_n)[None, :]
    step_pages = page_table[(s_idx // per_seq)[:, None], (nc - 1 - chunk) * g_n + (slot % g_n)[None, :]]
    step_pages = step_pages.reshape(n_steps * n).astype(i32)
    log2_per_seq = per_seq.bit_length() - 1
    assert per_seq == 1 << log2_per_seq

    def step_of(b, h, i):
        return (b * heads2 + h) * steps + i

    const2 = lambda b, h, i, sp, _: (0, 0)
    row3 = pl.BlockSpec(
        (None, 1, d), lambda b, h, i, sp, _: (lax.shift_right_logical(step_of(b, h, i), log2_per_seq), 0, 0))
    pool = pl.BlockSpec(memory_space=pl.ANY)
    grid_spec = pltpu.PrefetchScalarGridSpec(
        num_scalar_prefetch=2,
        grid=(batch, heads2, steps),
        in_specs=[pl.BlockSpec((2 * t, LANES), lambda b, h, i, pt, _: (b * steps + i, h)),
                  pl.BlockSpec((None, nq, LANES, t), lambda b, h, i, pt, _: (b, 0, h, 0)),
                  pl.BlockSpec((seq, LANES), lambda b, h, i, pt, _: (b, h)),
                  pl.BlockSpec((t, t), const2),
                  row3, row3, row3,
                  pl.BlockSpec((g_n * HEADS, PAGE), const2),
                  pl.BlockSpec((1, LANES), const2),
                  pl.BlockSpec((PAGE, PAGE), const2),
                  pl.BlockSpec((LANES, LANES), const2),
                  pl.BlockSpec((d, LANES), const2),
                  pool, pool],
        out_specs=(pl.BlockSpec((2 * t, LANES), lambda b, h, i, pt, _: (b * steps + i, h)), row3),
        scratch_shapes=[pltpu.VMEM((HEADS, d), BF16),
                        pltpu.VMEM((d, PAGE), F32),
                        pltpu.VMEM((HEADS, PAGE), F32),
                        pltpu.VMEM((g_n * HEADS, PAGE), F32),
                        pltpu.VMEM((2, n, d, PAGE), F32),
                        pltpu.VMEM((2, n, d, PAGE), F32),
                        pltpu.SemaphoreType.DMA((2,)),
                        pltpu.SemaphoreType.DMA((2,))],
    )
    o_p, o_s = pl.pallas_call(
        functools.partial(_sb_attention_kernel, per_seq=per_seq),
        out_shape=(jax.ShapeDtypeStruct((batch * seq, d), F32), jax.ShapeDtypeStruct((nb, 1, d), F32)),
        grid_spec=grid_spec,
        compiler_params=_params(("arbitrary", "arbitrary", "arbitrary")),
        name="sb_attention",
    )(step_pages, bias, q, kbt, vb, tri,
      q_s.reshape(nb, 1, d), k_new.reshape(nb, 1, d), v_new.reshape(nb, 1, d),
      bias_col, bias_row, u, ones, hsel, cache_kt, cache_vt)
    return o_p, o_s.reshape(nb, d)


def _att_out_rg_in(o, gate, x, mgate, shift, scale, g, wout_ref, win_ref):
    d = D_MODEL
    y = _dot((o * _silu(gate)).astype(BF16), wout_ref[...])
    x1 = x + mgate * y
    h = (_rmsnorm(x1, g) * (1.0 + scale) + shift).astype(BF16)
    return x1, _dot(h, win_ref[:, 0:d]), _dot(h, win_ref[:, d:2 * d])


def _rg_gates(xc, wa_ref, ba_ref, wx_ref, bx_ref, lam_ref):
    xcb = xc.astype(BF16)
    w = RG_BLOCK_W
    ra = jnp.concatenate([_dot(xcb[:, n * w:(n + 1) * w], wa_ref[n]) for n in range(RG_BLOCKS)], axis=-1)
    rx = jnp.concatenate([_dot(xcb[:, n * w:(n + 1) * w], wx_ref[n]) for n in range(RG_BLOCKS)], axis=-1)
    r = _sigmoid(ra + ba_ref[...])
    ig = _sigmoid(rx + bx_ref[...])
    lam = lam_ref[...]
    log_sig_lam = -(jnp.maximum(-lam, 0.0) + jnp.log1p(jnp.exp(-jnp.abs(lam))))
    log_a = RG_C * r * log_sig_lam
    a = jnp.exp(log_a)
    gap = -jnp.tanh(log_a) * (a * a + 1.0)
    u = (gap * lax.rsqrt(jnp.maximum(gap, F32_TINY))) * (ig * xc)
    return a, u


def _rg_finish(y, gbr, x1, mgate, wout_ref, fg_ref):
    out = _dot((y * _silu(gbr)).astype(BF16), wout_ref[...])
    return _rmsnorm(x1 + mgate * out, fg_ref[...])


def _rg_prompt_kernel(o_ref, gate_ref, xin_ref, mgate0_ref, shift_ref, scale_ref, g_ref, wao_ref, win_ref,
                      mgate_ref, cw_ref, cb_ref, wa_ref, ba_ref, wx_ref, bx_ref, lam_ref, wout_ref, fg_ref,
                      y_ref, tail_ref, hlast_ref, hist_s, a_s, u_s, h_s):
    tt = RG_TILE
    d = D_MODEL
    ti = pl.program_id(1)
    pad = SUBLANES

    @pl.when(ti == 0)
    def _():
        hist_s[...] = jnp.zeros((pad, d), F32)
        h_s[...] = jnp.zeros((SUBLANES, d), F32)

    x1, x, gbr = _att_out_rg_in(o_ref[...], gate_ref[...], xin_ref[...], mgate0_ref[...], shift_ref[...],
                                scale_ref[...], g_ref[...], wao_ref, win_ref)
    full = jnp.concatenate([hist_s[...], x], axis=0)
    xc = cb_ref[...] + cw_ref[3:4, :] * x
    for k in range(1, CONV_W):
        xc = xc + cw_ref[CONV_W - 1 - k:CONV_W - k, :] * pltpu.roll(full, k, axis=0)[pad:pad + tt, :]
    hist_s[...] = x[tt - pad:tt, :]

    a, u = _rg_gates(xc, wa_ref, ba_ref, wx_ref, bx_ref, lam_ref)
    a_s[...] = a
    u_s[...] = u
    sub = lax.broadcasted_iota(jnp.int32, (SUBLANES, d), 0)

    def group(i, h):
        r0 = pl.multiple_of(i * SUBLANES, SUBLANES)
        ag = a_s[pl.ds(r0, SUBLANES), :]
        ug = u_s[pl.ds(r0, SUBLANES), :]
        for s in (1, 2, 4):
            a_prev = jnp.where(sub >= s, pltpu.roll(ag, s, axis=0), 1.0)
            u_prev = jnp.where(sub >= s, pltpu.roll(ug, s, axis=0), 0.0)
            ug = ag * u_prev + ug
            ag = ag * a_prev
        hg = ag * h + ug
        u_s[pl.ds(r0, SUBLANES), :] = hg
        return jnp.broadcast_to(hg[SUBLANES - 1:SUBLANES, :], (SUBLANES, d))

    h = lax.fori_loop(0, tt // SUBLANES, group, h_s[...], unroll=4)
    h_s[...] = h

    y_ref[...] = _rg_finish(u_s[...], gbr, x1, mgate_ref[...], wout_ref, fg_ref)

    @pl.when(ti == pl.num_programs(1) - 1)
    def _():
        tail_ref[...] = x[tt - (CONV_W - 1):tt, :]
        hlast_ref[...] = h[0:1, :]


def _rg_prompt(o, gate, x, mgate0, shift, scale, g, w_att_out, w_rg_in,
               mgate, cw, cb, wa, ba, wx, bx, lam, wout, fg, *, batch, seq):
    d = D_MODEL
    tt = RG_TILE
    nt = seq // tt
    row = pl.BlockSpec((tt, d), lambda b, t: (b * nt + t, 0))
    per_seq = pl.BlockSpec((None, 1, d), lambda b, t: (b, 0, 0))
    vec = _resident((1, d))
    gates_w = _resident((RG_BLOCKS, RG_BLOCK_W, RG_BLOCK_W))
    return pl.pallas_call(
        _rg_prompt_kernel,
        out_shape=(jax.ShapeDtypeStruct((batch * seq, d), F32),
                   jax.ShapeDtypeStruct((batch, CONV_W - 1, d), F32),
                   jax.ShapeDtypeStruct((batch, 1, d), F32)),
        grid=(batch, nt),
        in_specs=[row, row, row, per_seq, per_seq, per_seq, vec, _resident((d, d)), _resident((d, 2 * d)),
                  per_seq,
                  _resident((CONV_W, d)),
                  vec, gates_w, vec, gates_w, vec, vec,
                  _resident((d, d)),
                  vec],
        out_specs=(row,
                   pl.BlockSpec((None, CONV_W - 1, d), lambda b, t: (b, 0, 0)),
                   per_seq),
        scratch_shapes=[pltpu.VMEM((SUBLANES, d), F32),
                        pltpu.VMEM((tt, d), F32),
                        pltpu.VMEM((tt, d), F32),
                        pltpu.VMEM((SUBLANES, d), F32)],
        compiler_params=_params(("arbitrary", "arbitrary")),
        name="rg_prompt",
    )(o, gate, x, mgate0, shift, scale, g, w_att_out, w_rg_in, mgate, cw, cb, wa, ba, wx, bx, lam, wout, fg)


def _rg_sample_kernel(o_ref, gate_ref, xin_ref, mgate0_ref, shift_ref, scale_ref, g_ref, wao_ref, win_ref,
                      mgate_ref, sconv_ref, sh_ref, cw_ref, cb_ref,
                      wa_ref, ba_ref, wx_ref, bx_ref, lam_ref, wout_ref, fg_ref,
                      y_ref, tail_ref, h_ref):
    x1, x, gbr = _att_out_rg_in(o_ref[...], gate_ref[...], xin_ref[...], mgate0_ref[...], shift_ref[...],
                                scale_ref[...], g_ref[...], wao_ref, win_ref)
    xc = cb_ref[...] + cw_ref[0:1, :] * sconv_ref[0]
    xc = xc + cw_ref[1:2, :] * sconv_ref[1]
    xc = xc + cw_ref[2:3, :] * sconv_ref[2]
    xc = xc + cw_ref[3:4, :] * x
    a, u = _rg_gates(xc, wa_ref, ba_ref, wx_ref, bx_ref, lam_ref)
    h = a * sh_ref[...] + u
    h_ref[...] = h
    tail_ref[0] = sconv_ref[1]
    tail_ref[1] = sconv_ref[2]
    tail_ref[2] = x
    y_ref[...] = _rg_finish(h, gbr, x1, mgate_ref[...], wout_ref, fg_ref)


def _rg_sample(o, gate, x, mgate0, shift, scale, g, w_att_out, w_rg_in,
               mgate, sconv, sh, cw, cb, wa, ba, wx, bx, lam, wout, fg):
    nb, d = x.shape
    sds = jax.ShapeDtypeStruct((nb, d), F32)
    return pl.pallas_call(
        _rg_sample_kernel,
        out_shape=(sds, jax.ShapeDtypeStruct((CONV_W - 1, nb, d), F32), sds),
        compiler_params=pltpu.CompilerParams(vmem_limit_bytes=VMEM_LIMIT),
        name="rg_sample",
    )(o, gate, x, mgate0, shift, scale, g, w_att_out, w_rg_in, mgate, sconv, sh,
      cw, cb, wa, ba, wx, bx, lam, wout, fg)


def kernel(x_prompt, x_sample, c_prompt, c_sample, cache_k, cache_v, page_table, state_conv, state_h,
           norm_g, mod_w, mod_b, att_w_in, att_bias, att_w_out, rg_w_in, rg_conv_w, rg_conv_b,
           rg_w_a, rg_b_a, rg_w_x, rg_b_x, rg_lambda, rg_w_out, final_g):
    batch, seq, d = x_prompt.shape
    nb, dec_seq, _ = x_sample.shape
    assert d == D_MODEL and dec_seq == 1 and mod_w.shape[0] == 2

    mod = _modulation(jnp.concatenate([c_prompt, c_sample], axis=0), mod_w, mod_b)

    def mod_parts(layer, lo, hi, shape):
        m = mod[layer, lo:hi]
        return tuple(m[:, i * d:(i + 1) * d].reshape(shape) for i in range(3))

    shift0p, scale0p, gate0p = mod_parts(0, 0, batch, (batch, 1, d))
    shift1p, scale1p, gate1p = mod_parts(1, 0, batch, (batch, 1, d))
    shift0s, scale0s, gate0s = mod_parts(0, batch, batch + nb, (nb, d))
    shift1s, scale1s, gate1s = mod_parts(1, batch, batch + nb, (nb, d))

    g0, g1 = norm_g[0].reshape(1, d), norm_g[1].reshape(1, d)
    w_att_in = att_w_in[0].astype(BF16)
    w_kv_t = jnp.swapaxes(att_w_in[0][:, d:3 * d], 0, 1).astype(BF16)
    w_att_out = att_w_out[0].astype(BF16)
    w_rg_in = rg_w_in[0].astype(BF16)
    w_rg_out = rg_w_out[0].astype(BF16)
    wa, wx = rg_w_a[0].astype(BF16), rg_w_x[0].astype(BF16)
    vec = lambda v: v.reshape(1, d)
    rg_consts = (rg_conv_w[0], vec(rg_conv_b[0]), wa, vec(rg_b_a[0]), wx, vec(rg_b_x[0]),
                 vec(rg_lambda[0]), w_rg_out, vec(final_g))
    bias = att_bias[0]

    xp = x_prompt.reshape(batch * seq, d)
    xs = x_sample.reshape(nb, d)
    q_p, kt_p, vt_p, gate_p, kbt_p, vb_p = _att_in_prompt(xp, shift0p, scale0p, g0, w_att_in, w_kv_t,
                                                          batch=batch, seq=seq)
    q_s, k_s, v_s, gate_s = _att_in_sample(xs, shift0s, scale0s, g0, w_att_in)
    pages_t = lambda cache: jnp.transpose(cache, (0, 1, 3, 4, 2)).reshape(-1, d, PAGE)
    o_p, o_s = _sb_attention(page_table, bias, q_p, kbt_p, vb_p, q_s, k_s, v_s,
                             pages_t(cache_k), pages_t(cache_v), batch=batch, seq=seq)

    y_p, tail_p, hlast_p = _rg_prompt(o_p, gate_p, xp, gate0p, shift1p, scale1p, g1, w_att_out, w_rg_in,
                                      gate1p, *rg_consts, batch=batch, seq=seq)
    y_s, tail_s, h_s = _rg_sample(o_s, gate_s, xs, gate0s, shift1s, scale1s, g1, w_att_out, w_rg_in,
                                  gate1s, jnp.swapaxes(state_conv[0], 0, 1), state_h[0], *rg_consts)

    def kv_rows(t):
        return jnp.transpose(t.reshape(1, batch, HEADS, HEAD_DIM, seq), (0, 1, 4, 2, 3))

    kv_s = (1, nb, 1, HEADS, HEAD_DIM)
    return (y_p.reshape(batch, seq, d), y_s.reshape(nb, 1, d),
            kv_rows(kt_p), kv_rows(vt_p), k_s.reshape(kv_s), v_s.reshape(kv_s),
            tail_p.reshape(1, batch, CONV_W - 1, d), hlast_p.reshape(1, batch, d),
            jnp.swapaxes(tail_s, 0, 1).reshape(1, nb, CONV_W - 1, d), h_s.reshape(1, nb, d))
```

```python
import functools

import jax
import jax.numpy as jnp
from jax import lax
from jax.experimental import pallas as pl
from jax.experimental.pallas import tpu as pltpu

F32 = jnp.float32
BF16 = jnp.bfloat16

D_MODEL = 1024
HEADS = 16
HEAD_DIM = 64
LANES = 128
SUBLANES = 8
HEADS_PER_LANE_TILE = LANES // HEAD_DIM
PAGE = 128
RG_BLOCKS = 4
RG_BLOCK_W = D_MODEL // RG_BLOCKS
CONV_W = 4
RG_C = 8.0
EPS = 1e-6
Q_SCALE = HEAD_DIM ** -0.5
LOG2_E = 1.4426950408889634
F32_TINY = 1.1754943508222875e-38

ROW_TILE = 512
ATT_TILE = 256
RG_TILE = 256
PAGES_PER_STEP = 8
CHUNKS_PER_STEP = 2
CHUNK_RING = 4
VMEM_LIMIT = 56 * 1024 * 1024

NT_DIMS = (((1,), (1,)), ((), ()))


def _dot(a, b):
    return jnp.dot(a, b, preferred_element_type=F32)


def _dot_nt(a, b):
    return lax.dot_general(a, b, NT_DIMS, preferred_element_type=F32)


def _sigmoid(x):
    return 0.5 * jnp.tanh(0.5 * x) + 0.5


def _silu(x):
    return x * _sigmoid(x)


def _neg_softplus(z):
    return -(jnp.maximum(z, 0.0) + jnp.log(1.0 + jnp.exp(-jnp.abs(z))))


def _split3(x):
    hi = x.astype(BF16)
    r = x - hi.astype(F32)
    mid = r.astype(BF16)
    lo = (r - mid.astype(F32)).astype(BF16)
    return hi, mid, lo


def _dot3(x, m):
    hi, mid, lo = _split3(x)
    return _dot(hi, m) + _dot(mid, m) + _dot(lo, m)


def _rmsnorm(x, g):
    ms = jnp.mean(x * x, axis=-1, keepdims=True)
    return (x * lax.rsqrt(ms + EPS)) * g


def _params(semantics):
    return pltpu.CompilerParams(dimension_semantics=semantics, vmem_limit_bytes=VMEM_LIMIT)


def _resident(shape):
    zeros = (0,) * len(shape)
    return pl.BlockSpec(shape, lambda *_: zeros, pipeline_mode=pl.Buffered(1))


def _mod_kernel(c_ref, w_ref, b_ref, o_ref):
    c = c_ref[...]
    o_ref[...] = _dot(_silu(c).astype(BF16), w_ref[...].astype(BF16)) + b_ref[...]


def _modulation(c_all, mod_w, mod_b):
    depth, d, n = mod_w.shape
    rows = c_all.shape[0]
    tn = 1024
    return pl.pallas_call(
        _mod_kernel,
        out_shape=jax.ShapeDtypeStruct((depth, rows, n), F32),
        grid=(depth, n // tn),
        in_specs=[
            pl.BlockSpec((rows, d), lambda l, j: (0, 0)),
            pl.BlockSpec((None, d, tn), lambda l, j: (l, 0, j)),
            pl.BlockSpec((None, 1, tn), lambda l, j: (l, 0, j)),
        ],
        out_specs=pl.BlockSpec((None, rows, tn), lambda l, j: (l, 0, j)),
        compiler_params=_params(("arbitrary", "arbitrary")),
        name="modulation",
    )(c_all, mod_w, mod_b.reshape(depth, 1, n))


def _att_in_hidden(x_ref, shift_ref, scale_ref, g_ref):
    h = _rmsnorm(x_ref[...], g_ref[...]) * (1.0 + scale_ref[...]) + shift_ref[...]
    return h.astype(BF16)


def _att_in_prompt_kernel(x_ref, shift_ref, scale_ref, g_ref, w_ref, wkv_t_ref,
                          q_ref, kt_ref, vt_ref, gate_ref, kbt_ref, vb_ref):
    d = D_MODEL
    t = ATT_TILE
    h = _att_in_hidden(x_ref, shift_ref, scale_ref, g_ref)
    q_ref[...] = (_dot(h, w_ref[:, 0:d]) * Q_SCALE).astype(BF16)
    kt = _dot_nt(wkv_t_ref[0:d, :], h)
    kt_ref[...] = kt
    for j in range(kbt_ref.shape[0]):
        kbt_ref[j] = kt[:, j * t:(j + 1) * t].astype(BF16)
    vt_ref[...] = _dot_nt(wkv_t_ref[d:2 * d, :], h)
    vb_ref[...] = _dot(h, w_ref[:, 2 * d:3 * d]).astype(BF16)
    gate_ref[...] = _dot(h, w_ref[:, 3 * d:4 * d])


def _att_in_prompt(x, shift, scale, g, w, wkv_t, *, batch, seq):
    r, d = x.shape
    tm = ROW_TILE
    t = ATT_TILE
    per_b = seq // tm
    row = pl.BlockSpec((tm, d), lambda i: (i, 0))
    mod = pl.BlockSpec((None, 1, d), lambda i: (i // per_b, 0, 0))
    feat = pl.BlockSpec((None, d, tm), lambda i: (i // per_b, 0, i % per_b))
    feat_tiles = pl.BlockSpec((None, tm // t, d, t), lambda i: (i // per_b, i % per_b, 0, 0))
    return pl.pallas_call(
        _att_in_prompt_kernel,
        out_shape=(
            jax.ShapeDtypeStruct((r, d), BF16),
            jax.ShapeDtypeStruct((batch, d, seq), F32),
            jax.ShapeDtypeStruct((batch, d, seq), F32),
            jax.ShapeDtypeStruct((r, d), F32),
            jax.ShapeDtypeStruct((batch, seq // t, d, t), BF16),
            jax.ShapeDtypeStruct((r, d), BF16),
        ),
        grid=(r // tm,),
        in_specs=[row, mod, mod, _resident((1, d)), _resident((d, 4 * d)), _resident((2 * d, d))],
        out_specs=(row, feat, feat, row, feat_tiles, row),
        compiler_params=_params(("arbitrary",)),
        name="att_in_prompt",
    )(x, shift, scale, g, w, wkv_t)


def _att_in_sample_kernel(x_ref, shift_ref, scale_ref, g_ref, w_ref, q_ref, k_ref, v_ref, gate_ref):
    d = D_MODEL
    h = _att_in_hidden(x_ref, shift_ref, scale_ref, g_ref)
    q_ref[...] = (_dot(h, w_ref[:, 0:d]) * Q_SCALE).astype(BF16)
    k_ref[...] = _dot(h, w_ref[:, d:2 * d])
    v_ref[...] = _dot(h, w_ref[:, 2 * d:3 * d])
    gate_ref[...] = _dot(h, w_ref[:, 3 * d:4 * d])


def _att_in_sample(x, shift, scale, g, w):
    r, d = x.shape
    sds = jax.ShapeDtypeStruct((r, d), F32)
    return pl.pallas_call(
        _att_in_sample_kernel,
        out_shape=(jax.ShapeDtypeStruct((r, d), BF16), sds, sds, sds),
        compiler_params=pltpu.CompilerParams(vmem_limit_bytes=VMEM_LIMIT),
        name="att_in_sample",
    )(x, shift, scale, g, w)


MASKED_LOG = -1e30


def _sb_scores(nqs, biases, k, masks):
    t = ATT_TILE
    nz_all = _dot(jnp.concatenate(nqs, axis=0), k)
    log_betas, log_keeps, totals = [], [], []
    for c in range(len(nqs)):
        nz = nz_all[c * t:(c + 1) * t] - biases[c]
        s = jnp.log(1.0 + jnp.exp2(jnp.abs(nz) * (-LOG2_E)))
        lk = jnp.minimum(nz, 0.0) - s
        lb = lk - nz
        if masks[c] is not None:
            lk = jnp.where(masks[c], lk, 0.0)
            lb = jnp.where(masks[c], lb, MASKED_LOG)
        log_betas.append(lb)
        log_keeps.append(lk)
        totals.append(jnp.sum(lk, axis=-1, keepdims=True))
    return log_betas, log_keeps, totals


def _sb_apply(scores, v, u, carries, accs):
    t = ATT_TILE
    log_betas, log_keeps, totals = scores
    n = len(log_betas)
    later_all = _dot(jnp.concatenate(log_keeps, axis=0), u)
    weights = [jnp.exp(log_betas[c] + later_all[c * t:(c + 1) * t] + carries[c]).astype(BF16)
               for c in range(n)]
    av_all = _dot(jnp.concatenate(weights, axis=0), v)
    accs = [accs[c] + av_all[c * t:(c + 1) * t] for c in range(n)]
    carries = [carries[c] + totals[c] for c in range(n)]
    return carries, accs


def _sb_prompt_step(bias_ref, q_ref, kt_ref, v_ref, tri_ref, o_ref, hp, i):
    t = ATT_TILE
    lane = lax.broadcasted_iota(jnp.int32, (1, LANES), 1)
    row = lax.broadcasted_iota(jnp.int32, (t, t), 0)
    col = lax.broadcasted_iota(jnp.int32, (t, t), 1)
    causal = col < row
    nq = -q_ref[...].astype(F32)
    tri = tri_ref[...].astype(F32)

    def values(j):
        return v_ref[pl.ds(pl.multiple_of(j * t, t), t), :]

    nqs, biases = [], []
    for hh in range(HEADS_PER_LANE_TILE):
        in_head = (lane >= hh * HEAD_DIM) & (lane < (hh + 1) * HEAD_DIM)
        masked = jnp.where(in_head, nq, 0.0).astype(BF16)
        nqs += [masked[0:t], masked[t:2 * t]]
        biases += [bias_ref[hp * HEADS_PER_LANE_TILE + hh]] * 2
    n = len(nqs)
    upper = [c for c in range(n) if c % 2 == 1]
    pick = lambda xs: [xs[c] for c in upper]

    carries = [jnp.zeros((t, 1), F32)] * n
    accs = [jnp.zeros((t, LANES), F32)] * n
    first = _sb_scores(pick(nqs), pick(biases), kt_ref[2 * i + 1], [causal] * len(upper))
    second = _sb_scores(nqs, biases, kt_ref[2 * i], [None if c in upper else causal for c in range(n)])
    cu, au = _sb_apply(first, values(2 * i + 1), tri, pick(carries), pick(accs))
    for idx, c in enumerate(upper):
        carries[c], accs[c] = cu[idx], au[idx]
    carries, accs = _sb_apply(second, values(2 * i), tri, carries, accs)

    def body(it, st):
        j = 2 * (i - it) - 1
        first = _sb_scores(nqs, biases, kt_ref[j], [None] * n)
        second = _sb_scores(nqs, biases, kt_ref[j - 1], [None] * n)
        cs, as_ = _sb_apply(first, values(j), tri, list(st[0]), list(st[1]))
        cs, as_ = _sb_apply(second, values(j - 1), tri, cs, as_)
        return tuple(cs), tuple(as_)

    carries, accs = lax.fori_loop(0, i, body, (tuple(carries), tuple(accs)))
    for r in range(2):
        o_ref[r * t:(r + 1) * t, :] = jnp.where(lane < HEAD_DIM, accs[r], accs[2 + r])


def _sb_sample_chunk(k_refs, v_refs, bias_ref, u_ref, ones, qbd_ref, acc_ref, carry_ref, a_ref):
    g_n = len(k_refs)
    hd = HEAD_DIM
    qbd = qbd_ref[...]
    z = jnp.concatenate([_dot(qbd, k_refs[g][...].astype(BF16)) for g in range(g_n)], axis=0)
    z = z + bias_ref[...]
    lk = _neg_softplus(z)
    hi, mid, lo = _split3(lk)
    u = u_ref[...]
    later = _dot(hi, u) + _dot(mid, u) + _dot(lo, u)
    tot = _dot(hi, ones) + _dot(mid, ones) + _dot(lo, ones)
    ex = z + lk + later
    off = carry_ref[...]
    for g in reversed(range(g_n)):
        sl = slice(g * HEADS, (g + 1) * HEADS)
        a_ref[sl, :] = jnp.exp(ex[sl] + off)
        off = off + tot[sl]
    carry_ref[...] = off
    for h in range(HEADS):
        feat = slice(h * hd, (h + 1) * hd)
        acc = acc_ref[feat, :]
        for g in range(g_n):
            acc = acc + a_ref[g * HEADS + h:g * HEADS + h + 1, :] * v_refs[g][feat, :]
        acc_ref[feat, :] = acc


def _sb_sample_step(first, last, q_ref, kn_ref, vn_ref, bias_ref, bias_row_ref, u_ref, ones_ref, hsel_ref,
                    chunks, o_ref, qbd_ref, acc_ref, carry_ref, a_ref, *, beside):
    d = D_MODEL
    ones = ones_ref[...]

    def new_token():
        mask_new = jnp.zeros((SUBLANES, LANES), jnp.int32) < jnp.zeros((SUBLANES, LANES), jnp.int32)
        qk = q_ref[...].astype(F32) * kn_ref[...].astype(BF16).astype(F32)
        z_new = _dot3(jnp.broadcast_to(qk, (SUBLANES, d)), hsel_ref[...]) + bias_row_ref[...]
        lk_new = _neg_softplus(z_new)
        a_new = jnp.where(mask_new, jnp.exp(z_new + lk_new), 0.0)
        return a_new, jnp.where(mask_new, lk_new, 0.0)

    @pl.when(first)
    def _():
        head = lax.broadcasted_iota(jnp.int32, (HEADS, d), 0)
        feat = lax.broadcasted_iota(jnp.int32, (HEADS, d), 1)
        own = (feat >= head * HEAD_DIM) & (feat < (head + 1) * HEAD_DIM)
        qbd_ref[...] = jnp.where(own, jnp.broadcast_to(q_ref[...].astype(F32), (HEADS, d)), 0.0).astype(BF16)
        acc_ref[...] = jnp.zeros((d, PAGE), F32)
        _, lk_new = new_token()
        head_diag = (lax.broadcasted_iota(jnp.int32, (HEADS, LANES), 0)
                     == lax.broadcasted_iota(jnp.int32, (HEADS, LANES), 1))
        carry_ref[...] = _dot3(jnp.where(head_diag, jnp.broadcast_to(lk_new[0:1, :], (HEADS, LANES)), 0.0), ones)

    for acquire, release in chunks:
        k_refs, v_refs = acquire()
        _sb_sample_chunk(k_refs, v_refs, bias_ref, u_ref, ones, qbd_ref, acc_ref, carry_ref, a_ref)
        release()
    beside()

    @pl.when(last)
    def _():
        hi_a, mid_a, lo_a = _split3(acc_ref[...])
        ones8 = jnp.ones((SUBLANES, PAGE), BF16)
        o = _dot_nt(ones8, hi_a) + _dot_nt(ones8, mid_a) + _dot_nt(ones8, lo_a)
        a_new, _ = new_token()
        o_new = _dot_nt(a_new.astype(BF16), hsel_ref[...]) * vn_ref[...]
        o_ref[...] = o[0:1, :] + o_new[0:1, :]


def _page_copy(cache_ref, buf_ref, sem_ref, page, slot, g):
    return pltpu.make_async_copy(cache_ref.at[page], buf_ref.at[slot, g], sem_ref.at[slot])


def _sb_attention_kernel(step_pages_ref, bias_ref, q_ref, kt_ref, v_ref, tri_ref,
                         qs_ref, kn_ref, vn_ref, bias_col_ref, bias_row_ref, u_ref, ones_ref, hsel_ref,
                         cache_k_ref, cache_v_ref, o_ref, os_ref,
                         qbd_ref, acc_ref, carry_ref, a_ref, kbuf_ref, vbuf_ref, ksem_ref, vsem_ref, *, per_seq):
    g_n, n_chunks, ring = PAGES_PER_STEP, CHUNKS_PER_STEP, CHUNK_RING
    b, hp, i = pl.program_id(0), pl.program_id(1), pl.program_id(2)
    n_steps = pl.num_programs(0) * pl.num_programs(1) * pl.num_programs(2)
    step = (b * pl.num_programs(1) + hp) * pl.num_programs(2) + i
    total = n_steps * n_chunks

    def start(chunk, slot):
        for g in range(g_n):
            page = step_pages_ref[chunk * g_n + g]
            _page_copy(cache_k_ref, kbuf_ref, ksem_ref, page, slot, g).start()
            _page_copy(cache_v_ref, vbuf_ref, vsem_ref, page, slot, g).start()

    def wait(slot):
        for g in range(g_n):
            _page_copy(cache_k_ref, kbuf_ref, ksem_ref, 0, slot, g).wait()
            _page_copy(cache_v_ref, vbuf_ref, vsem_ref, 0, slot, g).wait()

    @pl.when(step == 0)
    def _():
        for c in range(ring):
            start(c, c)

    def ring_chunk(c):
        chunk = step * n_chunks + c
        slot = chunk & (ring - 1)

        def acquire():
            wait(slot)
            return ([kbuf_ref.at[slot, g] for g in range(g_n)], [vbuf_ref.at[slot, g] for g in range(g_n)])

        def release():
            ahead = chunk + ring
            start(jnp.where(ahead >= total, ahead - total, ahead), slot)

        return acquire, release

    part = step & (per_seq - 1)
    prompt = functools.partial(_sb_prompt_step, bias_ref, q_ref, kt_ref, v_ref, tri_ref, o_ref, hp, i)
    _sb_sample_step(part == 0, part == per_seq - 1, qs_ref, kn_ref, vn_ref, bias_col_ref, bias_row_ref,
                    u_ref, ones_ref, hsel_ref, [ring_chunk(c) for c in range(n_chunks)], os_ref,
                    qbd_ref, acc_ref, carry_ref, a_ref, beside=prompt)

    @pl.when(step == n_steps - 1)
    def _():
        for slot in range(ring):
            wait(slot)


def _sb_attention(page_table, bias, q, kbt, vb, q_s, k_new, v_new, cache_kt, cache_vt, *, batch, seq):
    d = q.shape[1]
    t = ATT_TILE
    nq = seq // t
    steps = nq // 2
    nb = q_s.shape[0]
    g_n, n_chunks = PAGES_PER_STEP, CHUNKS_PER_STEP
    nc = page_table.shape[1] // g_n
    per_seq = nc // n_chunks
    heads2 = d // LANES
    assert batch * heads2 * steps == nb * per_seq
    i32 = jnp.int32

    idx = jnp.arange(t, dtype=i32)
    tri = (idx[:, None] > idx[None, :]).astype(BF16)
    key = jnp.arange(PAGE, dtype=i32)
    u = (key[:, None] > key[None, :]).astype(BF16)
    ones = jnp.ones((LANES, LANES), BF16)
    feat_head = jnp.arange(d, dtype=i32) // HEAD_DIM
    hsel = (feat_head[:, None] == jnp.arange(LANES, dtype=i32)[None, :]).astype(BF16)
    bias_col = jnp.broadcast_to(jnp.tile(bias, g_n)[:, None], (g_n * HEADS, PAGE))
    bias_row = jnp.pad(bias, (0, LANES - HEADS)).reshape(1, LANES)

    n = g_n * n_chunks
    n_steps = batch * heads2 * steps
    s_idx = jnp.arange(n_steps, dtype=i32)
    slot = jnp.arange(n, dtype=i32)
    chunk = (s_idx % per_seq)[:, None] * n_chunks + (slot // g_n)[None, :]
    step_pages = page_table[(s_idx // per_seq)[:, None], (nc - 1 - chunk) * g_n + (slot % g_n)[None, :]]
    step_pages = step_pages.reshape(n_steps * n).astype(i32)
    log2_per_seq = per_seq.bit_length() - 1
    assert per_seq == 1 << log2_per_seq

    def step_of(b, h, i):
        return (b * heads2 + h) * steps + i

    const2 = lambda b, h, i, sp, _: (0, 0)
    row3 = pl.BlockSpec(
        (None, 1, d), lambda b, h, i, sp, _: (lax.shift_right_logical(step_of(b, h, i), log2_per_seq), 0, 0))
    pool = pl.BlockSpec(memory_space=pl.ANY)
    grid_spec = pltpu.PrefetchScalarGridSpec(
        num_scalar_prefetch=2,
        grid=(batch, heads2, steps),
        in_specs=[pl.BlockSpec((2 * t, LANES), lambda b, h, i, pt, _: (b * steps + i, h)),
                  pl.BlockSpec((None, nq, LANES, t), lambda b, h, i, pt, _: (b, 0, h, 0)),
                  pl.BlockSpec((seq, LANES), lambda b, h, i, pt, _: (b, h)),
                  pl.BlockSpec((t, t), const2),
                  row3, row3, row3,
                  pl.BlockSpec((g_n * HEADS, PAGE), const2),
                  pl.BlockSpec((1, LANES), const2),
                  pl.BlockSpec((PAGE, PAGE), const2),
                  pl.BlockSpec((LANES, LANES), const2),
                  pl.BlockSpec((d, LANES), const2),
                  pool, pool],
        out_specs=(pl.BlockSpec((2 * t, LANES), lambda b, h, i, pt, _: (b * steps + i, h)), row3),
        scratch_shapes=[pltpu.VMEM((HEADS, d), BF16),
                        pltpu.VMEM((d, PAGE), F32),
                        pltpu.VMEM((HEADS, PAGE), F32),
                        pltpu.VMEM((g_n * HEADS, PAGE), F32),
                        pltpu.VMEM((CHUNK_RING, g_n, d, PAGE), F32),
                        pltpu.VMEM((CHUNK_RING, g_n, d, PAGE), F32),
                        pltpu.SemaphoreType.DMA((CHUNK_RING,)),
                        pltpu.SemaphoreType.DMA((CHUNK_RING,))],
    )
    o_p, o_s = pl.pallas_call(
        functools.partial(_sb_attention_kernel, per_seq=per_seq),
        out_shape=(jax.ShapeDtypeStruct((batch * seq, d), F32), jax.ShapeDtypeStruct((nb, 1, d), F32)),
        grid_spec=grid_spec,
        compiler_params=_params(("arbitrary", "arbitrary", "arbitrary")),
        name="sb_attention",
    )(step_pages, bias, q, kbt, vb, tri,
      q_s.reshape(nb, 1, d), k_new.reshape(nb, 1, d), v_new.reshape(nb, 1, d),
      bias_col, bias_row, u, ones, hsel, cache_kt, cache_vt)
    return o_p, o_s.reshape(nb, d)


def _att_out_rg_in(o, gate, x, mgate, shift, scale, g, wout_ref, win_ref):
    d = D_MODEL
    y = _dot((o * _silu(gate)).astype(BF16), wout_ref[...])
    x1 = x + mgate * y
    h = (_rmsnorm(x1, g) * (1.0 + scale) + shift).astype(BF16)
    return x1, _dot(h, win_ref[:, 0:d]), _dot(h, win_ref[:, d:2 * d])


def _rg_gates(xc, wa_ref, ba_ref, wx_ref, bx_ref, lam_ref):
    xcb = xc.astype(BF16)
    w = RG_BLOCK_W
    ra = jnp.concatenate([_dot(xcb[:, n * w:(n + 1) * w], wa_ref[n]) for n in range(RG_BLOCKS)], axis=-1)
    rx = jnp.concatenate([_dot(xcb[:, n * w:(n + 1) * w], wx_ref[n]) for n in range(RG_BLOCKS)], axis=-1)
    r = _sigmoid(ra + ba_ref[...])
    ig = _sigmoid(rx + bx_ref[...])
    lam = lam_ref[...]
    log_sig_lam = -(jnp.maximum(-lam, 0.0) + jnp.log1p(jnp.exp(-jnp.abs(lam))))
    log_a = RG_C * r * log_sig_lam
    a = jnp.exp(log_a)
    gap = -jnp.tanh(log_a) * (a * a + 1.0)
    u = (gap * lax.rsqrt(jnp.maximum(gap, F32_TINY))) * (ig * xc)
    return a, u


def _rg_finish(y, gbr, x1, mgate, wout_ref, fg_ref):
    out = _dot((y * _silu(gbr)).astype(BF16), wout_ref[...])
    return _rmsnorm(x1 + mgate * out, fg_ref[...])


def _rg_prompt_kernel(o_ref, gate_ref, xin_ref, mgate0_ref, shift_ref, scale_ref, g_ref, wao_ref, win_ref,
                      mgate_ref, cw_ref, cb_ref, wa_ref, ba_ref, wx_ref, bx_ref, lam_ref, wout_ref, fg_ref,
                      y_ref, tail_ref, hlast_ref, hist_s, a_s, u_s, h_s):
    tt = RG_TILE
    d = D_MODEL
    ti = pl.program_id(1)
    pad = SUBLANES

    @pl.when(ti == 0)
    def _():
        hist_s[...] = jnp.zeros((pad, d), F32)
        h_s[...] = jnp.zeros((SUBLANES, d), F32)

    x1, x, gbr = _att_out_rg_in(o_ref[...], gate_ref[...], xin_ref[...], mgate0_ref[...], shift_ref[...],
                                scale_ref[...], g_ref[...], wao_ref, win_ref)
    full = jnp.concatenate([hist_s[...], x], axis=0)
    xc = cb_ref[...] + cw_ref[3:4, :] * x
    for k in range(1, CONV_W):
        xc = xc + cw_ref[CONV_W - 1 - k:CONV_W - k, :] * pltpu.roll(full, k, axis=0)[pad:pad + tt, :]
    hist_s[...] = x[tt - pad:tt, :]

    a, u = _rg_gates(xc, wa_ref, ba_ref, wx_ref, bx_ref, lam_ref)
    a_s[...] = a
    u_s[...] = u
    sub = lax.broadcasted_iota(jnp.int32, (SUBLANES, d), 0)

    def group(i, h):
        r0 = pl.multiple_of(i * SUBLANES, SUBLANES)
        ag = a_s[pl.ds(r0, SUBLANES), :]
        ug = u_s[pl.ds(r0, SUBLANES), :]
        for s in (1, 2, 4):
            a_prev = jnp.where(sub >= s, pltpu.roll(ag, s, axis=0), 1.0)
            u_prev = jnp.where(sub >= s, pltpu.roll(ug, s, axis=0), 0.0)
            ug = ag * u_prev + ug
            ag = ag * a_prev
        hg = ag * h + ug
        u_s[pl.ds(r0, SUBLANES), :] = hg
        return jnp.broadcast_to(hg[SUBLANES - 1:SUBLANES, :], (SUBLANES, d))

    h = lax.fori_loop(0, tt // SUBLANES, group, h_s[...], unroll=4)
    h_s[...] = h

    y_ref[...] = _rg_finish(u_s[...], gbr, x1, mgate_ref[...], wout_ref, fg_ref)

    @pl.when(ti == pl.num_programs(1) - 1)
    def _():
        tail_ref[...] = x[tt - (CONV_W - 1):tt, :]
        hlast_ref[...] = h[0:1, :]


def _rg_prompt(o, gate, x, mgate0, shift, scale, g, w_att_out, w_rg_in,
               mgate, cw, cb, wa, ba, wx, bx, lam, wout, fg, *, batch, seq):
    d = D_MODEL
    tt = RG_TILE
    nt = seq // tt
    row = pl.BlockSpec((tt, d), lambda b, t: (b * nt + t, 0))
    per_seq = pl.BlockSpec((None, 1, d), lambda b, t: (b, 0, 0))
    vec = _resident((1, d))
    gates_w = _resident((RG_BLOCKS, RG_BLOCK_W, RG_BLOCK_W))
    return pl.pallas_call(
        _rg_prompt_kernel,
        out_shape=(jax.ShapeDtypeStruct((batch * seq, d), F32),
                   jax.ShapeDtypeStruct((batch, CONV_W - 1, d), F32),
                   jax.ShapeDtypeStruct((batch, 1, d), F32)),
        grid=(batch, nt),
        in_specs=[row, row, row, per_seq, per_seq, per_seq, vec, _resident((d, d)), _resident((d, 2 * d)),
                  per_seq,
                  _resident((CONV_W, d)),
                  vec, gates_w, vec, gates_w, vec, vec,
                  _resident((d, d)),
                  vec],
        out_specs=(row,
                   pl.BlockSpec((None, CONV_W - 1, d), lambda b, t: (b, 0, 0)),
                   per_seq),
        scratch_shapes=[pltpu.VMEM((SUBLANES, d), F32),
                        pltpu.VMEM((tt, d), F32),
                        pltpu.VMEM((tt, d), F32),
                        pltpu.VMEM((SUBLANES, d), F32)],
        compiler_params=_params(("arbitrary", "arbitrary")),
        name="rg_prompt",
    )(o, gate, x, mgate0, shift, scale, g, w_att_out, w_rg_in, mgate, cw, cb, wa, ba, wx, bx, lam, wout, fg)


def _rg_sample_kernel(o_ref, gate_ref, xin_ref, mgate0_ref, shift_ref, scale_ref, g_ref, wao_ref, win_ref,
                      mgate_ref, sconv_ref, sh_ref, cw_ref, cb_ref,
                      wa_ref, ba_ref, wx_ref, bx_ref, lam_ref, wout_ref, fg_ref,
                      y_ref, tail_ref, h_ref):
    x1, x, gbr = _att_out_rg_in(o_ref[...], gate_ref[...], xin_ref[...], mgate0_ref[...], shift_ref[...],
                                scale_ref[...], g_ref[...], wao_ref, win_ref)
    xc = cb_ref[...] + cw_ref[0:1, :] * sconv_ref[0]
    xc = xc + cw_ref[1:2, :] * sconv_ref[1]
    xc = xc + cw_ref[2:3, :] * sconv_ref[2]
    xc = xc + cw_ref[3:4, :] * x
    a, u = _rg_gates(xc, wa_ref, ba_ref, wx_ref, bx_ref, lam_ref)
    h = a * sh_ref[...] + u
    h_ref[...] = h
    tail_ref[0] = sconv_ref[1]
    tail_ref[1] = sconv_ref[2]
    tail_ref[2] = x
    y_ref[...] = _rg_finish(h, gbr, x1, mgate_ref[...], wout_ref, fg_ref)


def _rg_sample(o, gate, x, mgate0, shift, scale, g, w_att_out, w_rg_in,
               mgate, sconv, sh, cw, cb, wa, ba, wx, bx, lam, wout, fg):
    nb, d = x.shape
    sds = jax.ShapeDtypeStruct((nb, d), F32)
    return pl.pallas_call(
        _rg_sample_kernel,
        out_shape=(sds, jax.ShapeDtypeStruct((CONV_W - 1, nb, d), F32), sds),
        compiler_params=pltpu.CompilerParams(vmem_limit_bytes=VMEM_LIMIT),
        name="rg_sample",
    )(o, gate, x, mgate0, shift, scale, g, w_att_out, w_rg_in, mgate, sconv, sh,
      cw, cb, wa, ba, wx, bx, lam, wout, fg)


def kernel(x_prompt, x_sample, c_prompt, c_sample, cache_k, cache_v, page_table, state_conv, state_h,
           norm_g, mod_w, mod_b, att_w_in, att_bias, att_w_out, rg_w_in, rg_conv_w, rg_conv_b,
           rg_w_a, rg_b_a, rg_w_x, rg_b_x, rg_lambda, rg_w_out, final_g):
    batch, seq, d = x_prompt.shape
    nb, dec_seq, _ = x_sample.shape
    assert d == D_MODEL and dec_seq == 1 and mod_w.shape[0] == 2

    mod = _modulation(jnp.concatenate([c_prompt, c_sample], axis=0), mod_w, mod_b)

    def mod_parts(layer, lo, hi, shape):
        m = mod[layer, lo:hi]
        return tuple(m[:, i * d:(i + 1) * d].reshape(shape) for i in range(3))

    shift0p, scale0p, gate0p = mod_parts(0, 0, batch, (batch, 1, d))
    shift1p, scale1p, gate1p = mod_parts(1, 0, batch, (batch, 1, d))
    shift0s, scale0s, gate0s = mod_parts(0, batch, batch + nb, (nb, d))
    shift1s, scale1s, gate1s = mod_parts(1, batch, batch + nb, (nb, d))

    g0, g1 = norm_g[0].reshape(1, d), norm_g[1].reshape(1, d)
    w_att_in = att_w_in[0].astype(BF16)
    w_kv_t = jnp.swapaxes(att_w_in[0][:, d:3 * d], 0, 1).astype(BF16)
    w_att_out = att_w_out[0].astype(BF16)
    w_rg_in = rg_w_in[0].astype(BF16)
    w_rg_out = rg_w_out[0].astype(BF16)
    wa, wx = rg_w_a[0].astype(BF16), rg_w_x[0].astype(BF16)
    vec = lambda v: v.reshape(1, d)
    rg_consts = (rg_conv_w[0], vec(rg_conv_b[0]), wa, vec(rg_b_a[0]), wx, vec(rg_b_x[0]),
                 vec(rg_lambda[0]), w_rg_out, vec(final_g))
    bias = att_bias[0]

    xp = x_prompt.reshape(batch * seq, d)
    xs = x_sample.reshape(nb, d)
    q_p, kt_p, vt_p, gate_p, kbt_p, vb_p = _att_in_prompt(xp, shift0p, scale0p, g0, w_att_in, w_kv_t,
                                                          batch=batch, seq=seq)
    q_s, k_s, v_s, gate_s = _att_in_sample(xs, shift0s, scale0s, g0, w_att_in)
    pages_t = lambda cache: jnp.transpose(cache, (0, 1, 3, 4, 2)).reshape(-1, d, PAGE)
    o_p, o_s = _sb_attention(page_table, bias, q_p, kbt_p, vb_p, q_s, k_s, v_s,
                             pages_t(cache_k), pages_t(cache_v), batch=batch, seq=seq)

    y_p, tail_p, hlast_p = _rg_prompt(o_p, gate_p, xp, gate0p, shift1p, scale1p, g1, w_att_out, w_rg_in,
                                      gate1p, *rg_consts, batch=batch, seq=seq)
    y_s, tail_s, h_s = _rg_sample(o_s, gate_s, xs, gate0s, shift1s, scale1s, g1, w_att_out, w_rg_in,
                                  gate1s, jnp.swapaxes(state_conv[0], 0, 1), state_h[0], *rg_consts)

    def kv_rows(t):
        return jnp.transpose(t.reshape(1, batch, HEADS, HEAD_DIM, seq), (0, 1, 4, 2, 3))

    kv_s = (1, nb, 1, HEADS, HEAD_DIM)
    return (y_p.reshape(batch, seq, d), y_s.reshape(nb, 1, d),
            kv_rows(kt_p), kv_rows(vt_p), k_s.reshape(kv_s), v_s.reshape(kv_s),
            tail_p.reshape(1, batch, CONV_W - 1, d), hlast_p.reshape(1, batch, d),
            jnp.swapaxes(tail_s, 0, 1).reshape(1, nb, CONV_W - 1, d), h_s.reshape(1, nb, d))
```

```python
import functools

import jax
import jax.numpy as jnp
from jax import lax
from jax.experimental import pallas as pl
from jax.experimental.pallas import tpu as pltpu

F32 = jnp.float32
BF16 = jnp.bfloat16

D_MODEL = 1024
HEADS = 16
HEAD_DIM = 64
LANES = 128
SUBLANES = 8
HEADS_PER_LANE_TILE = LANES // HEAD_DIM
PAGE = 128
RG_BLOCKS = 4
RG_BLOCK_W = D_MODEL // RG_BLOCKS
CONV_W = 4
RG_C = 8.0
EPS = 1e-6
Q_SCALE = HEAD_DIM ** -0.5
LOG2_E = 1.4426950408889634
F32_TINY = 1.1754943508222875e-38

ROW_TILE = 512
ATT_TILE = 256
RG_TILE = 256
PAGES_PER_STEP = 8
CHUNKS_PER_STEP = 2
CHUNK_RING = 4
VMEM_LIMIT = 56 * 1024 * 1024

NT_DIMS = (((1,), (1,)), ((), ()))


def _dot(a, b):
    return jnp.dot(a, b, preferred_element_type=F32)


def _dot_nt(a, b):
    return lax.dot_general(a, b, NT_DIMS, preferred_element_type=F32)


def _sigmoid(x):
    return 0.5 * jnp.tanh(0.5 * x) + 0.5


def _silu(x):
    return x * _sigmoid(x)


def _neg_softplus(z):
    return -(jnp.maximum(z, 0.0) + jnp.log(1.0 + jnp.exp(-jnp.abs(z))))


def _split3(x):
    hi = x.astype(BF16)
    r = x - hi.astype(F32)
    mid = r.astype(BF16)
    lo = (r - mid.astype(F32)).astype(BF16)
    return hi, mid, lo


def _dot3(x, m):
    hi, mid, lo = _split3(x)
    return _dot(hi, m) + _dot(mid, m) + _dot(lo, m)


def _rmsnorm(x, g):
    ms = jnp.mean(x * x, axis=-1, keepdims=True)
    return (x * lax.rsqrt(ms + EPS)) * g


def _params(semantics):
    return pltpu.CompilerParams(dimension_semantics=semantics, vmem_limit_bytes=VMEM_LIMIT)


def _resident(shape):
    zeros = (0,) * len(shape)
    return pl.BlockSpec(shape, lambda *_: zeros, pipeline_mode=pl.Buffered(1))


def _mod_kernel(c_ref, w_ref, b_ref, o_ref):
    c = c_ref[...]
    o_ref[...] = _dot(_silu(c).astype(BF16), w_ref[...].astype(BF16)) + b_ref[...]


def _modulation(c_all, mod_w, mod_b):
    depth, d, n = mod_w.shape
    rows = c_all.shape[0]
    tn = 1024
    return pl.pallas_call(
        _mod_kernel,
        out_shape=jax.ShapeDtypeStruct((depth, rows, n), F32),
        grid=(depth, n // tn),
        in_specs=[
            pl.BlockSpec((rows, d), lambda l, j: (0, 0)),
            pl.BlockSpec((None, d, tn), lambda l, j: (l, 0, j)),
            pl.BlockSpec((None, 1, tn), lambda l, j: (l, 0, j)),
        ],
        out_specs=pl.BlockSpec((None, rows, tn), lambda l, j: (l, 0, j)),
        compiler_params=_params(("arbitrary", "arbitrary")),
        name="modulation",
    )(c_all, mod_w, mod_b.reshape(depth, 1, n))


def _att_in_hidden(x_ref, shift_ref, scale_ref, g_ref):
    h = _rmsnorm(x_ref[...], g_ref[...]) * (1.0 + scale_ref[...]) + shift_ref[...]
    return h.astype(BF16)


def _att_in_prompt_kernel(x_ref, shift_ref, scale_ref, g_ref, w_ref, wkv_t_ref,
                          q_ref, kt_ref, vt_ref, gate_ref, kbt_ref, vbt_ref):
    d = D_MODEL
    t = ATT_TILE
    h = _att_in_hidden(x_ref, shift_ref, scale_ref, g_ref)
    q_ref[...] = (_dot(h, w_ref[:, 0:d]) * Q_SCALE).astype(BF16)
    kt = _dot_nt(wkv_t_ref[0:d, :], h)
    kt_ref[...] = kt
    vt = _dot_nt(wkv_t_ref[d:2 * d, :], h)
    vt_ref[...] = vt
    for j in range(kbt_ref.shape[0]):
        kbt_ref[j] = kt[:, j * t:(j + 1) * t].astype(BF16)
        vbt_ref[j] = vt[:, j * t:(j + 1) * t].astype(BF16)
    gate_ref[...] = _dot(h, w_ref[:, 3 * d:4 * d])


def _att_in_prompt(x, shift, scale, g, w, wkv_t, *, batch, seq):
    r, d = x.shape
    tm = ROW_TILE
    t = ATT_TILE
    per_b = seq // tm
    row = pl.BlockSpec((tm, d), lambda i: (i, 0))
    mod = pl.BlockSpec((None, 1, d), lambda i: (i // per_b, 0, 0))
    feat = pl.BlockSpec((None, d, tm), lambda i: (i // per_b, 0, i % per_b))
    feat_tiles = pl.BlockSpec((None, tm // t, d, t), lambda i: (i // per_b, i % per_b, 0, 0))
    return pl.pallas_call(
        _att_in_prompt_kernel,
        out_shape=(
            jax.ShapeDtypeStruct((r, d), BF16),
            jax.ShapeDtypeStruct((batch, d, seq), F32),
            jax.ShapeDtypeStruct((batch, d, seq), F32),
            jax.ShapeDtypeStruct((r, d), F32),
            jax.ShapeDtypeStruct((batch, seq // t, d, t), BF16),
            jax.ShapeDtypeStruct((batch, seq // t, d, t), BF16),
        ),
        grid=(r // tm,),
        in_specs=[row, mod, mod, _resident((1, d)), _resident((d, 4 * d)), _resident((2 * d, d))],
        out_specs=(row, feat, feat, row, feat_tiles, feat_tiles),
        compiler_params=_params(("arbitrary",)),
        name="att_in_prompt",
    )(x, shift, scale, g, w, wkv_t)


def _att_in_sample_kernel(x_ref, shift_ref, scale_ref, g_ref, w_ref, q_ref, k_ref, v_ref, gate_ref):
    d = D_MODEL
    h = _att_in_hidden(x_ref, shift_ref, scale_ref, g_ref)
    q_ref[...] = (_dot(h, w_ref[:, 0:d]) * Q_SCALE).astype(BF16)
    k_ref[...] = _dot(h, w_ref[:, d:2 * d])
    v_ref[...] = _dot(h, w_ref[:, 2 * d:3 * d])
    gate_ref[...] = _dot(h, w_ref[:, 3 * d:4 * d])


def _att_in_sample(x, shift, scale, g, w):
    r, d = x.shape
    sds = jax.ShapeDtypeStruct((r, d), F32)
    return pl.pallas_call(
        _att_in_sample_kernel,
        out_shape=(jax.ShapeDtypeStruct((r, d), BF16), sds, sds, sds),
        compiler_params=pltpu.CompilerParams(vmem_limit_bytes=VMEM_LIMIT),
        name="att_in_sample",
    )(x, shift, scale, g, w)


MASKED_LOG = -1e30


def _sb_scores(nqs, biases, k, masks):
    t = ATT_TILE
    nz_all = _dot(jnp.concatenate(nqs, axis=0), k)
    log_betas, log_keeps, totals = [], [], []
    for c in range(len(nqs)):
        nz = nz_all[c * t:(c + 1) * t] - biases[c]
        s = jnp.log(1.0 + jnp.exp2(jnp.abs(nz) * (-LOG2_E)))
        lk = jnp.minimum(nz, 0.0) - s
        lb = lk - nz
        if masks[c] is not None:
            lk = jnp.where(masks[c], lk, 0.0)
            lb = jnp.where(masks[c], lb, MASKED_LOG)
        log_betas.append(lb)
        log_keeps.append(lk)
        totals.append(jnp.sum(lk, axis=-1, keepdims=True))
    return log_betas, log_keeps, totals


def _sb_apply(scores, v, u, carries, accs):
    t = ATT_TILE
    log_betas, log_keeps, totals = scores
    n = len(log_betas)
    later_all = _dot(jnp.concatenate(log_keeps, axis=0), u)
    weights = [jnp.exp(log_betas[c] + later_all[c * t:(c + 1) * t] + carries[c]).astype(BF16)
               for c in range(n)]
    av_all = _dot_nt(jnp.concatenate(weights, axis=0), v)
    accs = [accs[c] + av_all[c * t:(c + 1) * t] for c in range(n)]
    carries = [carries[c] + totals[c] for c in range(n)]
    return carries, accs


def _sb_prompt_step(bias_ref, q_ref, kt_ref, v_ref, tri_ref, o_ref, hp, i):
    t = ATT_TILE
    lane = lax.broadcasted_iota(jnp.int32, (1, LANES), 1)
    row = lax.broadcasted_iota(jnp.int32, (t, t), 0)
    col = lax.broadcasted_iota(jnp.int32, (t, t), 1)
    causal = col < row
    nq = -q_ref[...].astype(F32)
    tri = tri_ref[...].astype(F32)

    def values(j):
        return v_ref[j]

    nqs, biases = [], []
    for hh in range(HEADS_PER_LANE_TILE):
        in_head = (lane >= hh * HEAD_DIM) & (lane < (hh + 1) * HEAD_DIM)
        masked = jnp.where(in_head, nq, 0.0).astype(BF16)
        nqs += [masked[0:t], masked[t:2 * t]]
        biases += [bias_ref[hp * HEADS_PER_LANE_TILE + hh]] * 2
    n = len(nqs)
    upper = [c for c in range(n) if c % 2 == 1]
    pick = lambda xs: [xs[c] for c in upper]

    carries = [jnp.zeros((t, 1), F32)] * n
    accs = [jnp.zeros((t, LANES), F32)] * n
    first = _sb_scores(pick(nqs), pick(biases), kt_ref[2 * i + 1], [causal] * len(upper))
    second = _sb_scores(nqs, biases, kt_ref[2 * i], [None if c in upper else causal for c in range(n)])
    cu, au = _sb_apply(first, values(2 * i + 1), tri, pick(carries), pick(accs))
    for idx, c in enumerate(upper):
        carries[c], accs[c] = cu[idx], au[idx]
    carries, accs = _sb_apply(second, values(2 * i), tri, carries, accs)

    def body(it, st):
        j = 2 * (i - it) - 1
        first = _sb_scores(nqs, biases, kt_ref[j], [None] * n)
        second = _sb_scores(nqs, biases, kt_ref[j - 1], [None] * n)
        cs, as_ = _sb_apply(first, values(j), tri, list(st[0]), list(st[1]))
        cs, as_ = _sb_apply(second, values(j - 1), tri, cs, as_)
        return tuple(cs), tuple(as_)

    carries, accs = lax.fori_loop(0, i, body, (tuple(carries), tuple(accs)))
    for r in range(2):
        o_ref[r * t:(r + 1) * t, :] = jnp.where(lane < HEAD_DIM, accs[r], accs[2 + r])


def _sb_sample_chunk(k_refs, v_refs, bias_ref, u_ref, ones, qbd_ref, acc_ref, carry_ref, a_ref):
    g_n = len(k_refs)
    hd = HEAD_DIM
    qbd = qbd_ref[...]
    z = jnp.concatenate([_dot(qbd, k_refs[g][...].astype(BF16)) for g in range(g_n)], axis=0)
    z = z + bias_ref[...]
    lk = _neg_softplus(z)
    hi, mid, lo = _split3(lk)
    u = u_ref[...]
    later = _dot(hi, u) + _dot(mid, u) + _dot(lo, u)
    tot = _dot(hi, ones) + _dot(mid, ones) + _dot(lo, ones)
    ex = z + lk + later
    off = carry_ref[...]
    for g in reversed(range(g_n)):
        sl = slice(g * HEADS, (g + 1) * HEADS)
        a_ref[sl, :] = jnp.exp(ex[sl] + off)
        off = off + tot[sl]
    carry_ref[...] = off
    for h in range(HEADS):
        feat = slice(h * hd, (h + 1) * hd)
        acc = acc_ref[feat, :]
        for g in range(g_n):
            acc = acc + a_ref[g * HEADS + h:g * HEADS + h + 1, :] * v_refs[g][feat, :]
        acc_ref[feat, :] = acc


def _sb_sample_step(first, last, q_ref, kn_ref, vn_ref, bias_ref, bias_row_ref, u_ref, ones_ref, hsel_ref,
                    chunks, o_ref, qbd_ref, acc_ref, carry_ref, a_ref, *, beside):
    d = D_MODEL
    ones = ones_ref[...]

    def new_token():
        mask_new = jnp.zeros((SUBLANES, LANES), jnp.int32) < jnp.zeros((SUBLANES, LANES), jnp.int32)
        qk = q_ref[...].astype(F32) * kn_ref[...].astype(BF16).astype(F32)
        z_new = _dot3(jnp.broadcast_to(qk, (SUBLANES, d)), hsel_ref[...]) + bias_row_ref[...]
        lk_new = _neg_softplus(z_new)
        a_new = jnp.where(mask_new, jnp.exp(z_new + lk_new), 0.0)
        return a_new, jnp.where(mask_new, lk_new, 0.0)

    @pl.when(first)
    def _():
        head = lax.broadcasted_iota(jnp.int32, (HEADS, d), 0)
        feat = lax.broadcasted_iota(jnp.int32, (HEADS, d), 1)
        own = (feat >= head * HEAD_DIM) & (feat < (head + 1) * HEAD_DIM)
        qbd_ref[...] = jnp.where(own, jnp.broadcast_to(q_ref[...].astype(F32), (HEADS, d)), 0.0).astype(BF16)
        acc_ref[...] = jnp.zeros((d, PAGE), F32)
        _, lk_new = new_token()
        head_diag = (lax.broadcasted_iota(jnp.int32, (HEADS, LANES), 0)
                     == lax.broadcasted_iota(jnp.int32, (HEADS, LANES), 1))
        carry_ref[...] = _dot3(jnp.where(head_diag, jnp.broadcast_to(lk_new[0:1, :], (HEADS, LANES)), 0.0), ones)

    for acquire, release in chunks:
        k_refs, v_refs = acquire()
        _sb_sample_chunk(k_refs, v_refs, bias_ref, u_ref, ones, qbd_ref, acc_ref, carry_ref, a_ref)
        release()
    beside()

    @pl.when(last)
    def _():
        hi_a, mid_a, lo_a = _split3(acc_ref[...])
        ones8 = jnp.ones((SUBLANES, PAGE), BF16)
        o = _dot_nt(ones8, hi_a) + _dot_nt(ones8, mid_a) + _dot_nt(ones8, lo_a)
        a_new, _ = new_token()
        o_new = _dot_nt(a_new.astype(BF16), hsel_ref[...]) * vn_ref[...]
        o_ref[...] = o[0:1, :] + o_new[0:1, :]


def _page_copy(cache_ref, buf_ref, sem_ref, page, slot, g):
    return pltpu.make_async_copy(cache_ref.at[page], buf_ref.at[slot, g], sem_ref.at[slot])


def _sb_attention_kernel(step_pages_ref, bias_ref, q_ref, kt_ref, v_ref, tri_ref,
                         qs_ref, kn_ref, vn_ref, bias_col_ref, bias_row_ref, u_ref, ones_ref, hsel_ref,
                         cache_k_ref, cache_v_ref, o_ref, os_ref,
                         qbd_ref, acc_ref, carry_ref, a_ref, kbuf_ref, vbuf_ref, ksem_ref, vsem_ref, *, per_seq):
    g_n, n_chunks, ring = PAGES_PER_STEP, CHUNKS_PER_STEP, CHUNK_RING
    b, hp, i = pl.program_id(0), pl.program_id(1), pl.program_id(2)
    n_steps = pl.num_programs(0) * pl.num_programs(1) * pl.num_programs(2)
    step = (b * pl.num_programs(1) + hp) * pl.num_programs(2) + i
    total = n_steps * n_chunks

    def start(chunk, slot):
        for g in range(g_n):
            page = step_pages_ref[chunk * g_n + g]
            _page_copy(cache_k_ref, kbuf_ref, ksem_ref, page, slot, g).start()
            _page_copy(cache_v_ref, vbuf_ref, vsem_ref, page, slot, g).start()

    def wait(slot):
        for g in range(g_n):
            _page_copy(cache_k_ref, kbuf_ref, ksem_ref, 0, slot, g).wait()
            _page_copy(cache_v_ref, vbuf_ref, vsem_ref, 0, slot, g).wait()

    @pl.when(step == 0)
    def _():
        for c in range(ring):
            start(c, c)

    def ring_chunk(c):
        chunk = step * n_chunks + c
        slot = chunk & (ring - 1)

        def acquire():
            wait(slot)
            return ([kbuf_ref.at[slot, g] for g in range(g_n)], [vbuf_ref.at[slot, g] for g in range(g_n)])

        def release():
            ahead = chunk + ring
            start(jnp.where(ahead >= total, ahead - total, ahead), slot)

        return acquire, release

    part = step & (per_seq - 1)
    prompt = functools.partial(_sb_prompt_step, bias_ref, q_ref, kt_ref, v_ref, tri_ref, o_ref, hp, i)
    _sb_sample_step(part == 0, part == per_seq - 1, qs_ref, kn_ref, vn_ref, bias_col_ref, bias_row_ref,
                    u_ref, ones_ref, hsel_ref, [ring_chunk(c) for c in range(n_chunks)], os_ref,
                    qbd_ref, acc_ref, carry_ref, a_ref, beside=prompt)

    @pl.when(step == n_steps - 1)
    def _():
        for slot in range(ring):
            wait(slot)


def _sb_attention(page_table, bias, q, kbt, vb, q_s, k_new, v_new, cache_kt, cache_vt, *, batch, seq):
    d = q.shape[1]
    t = ATT_TILE
    nq = seq // t
    steps = nq // 2
    nb = q_s.shape[0]
    g_n, n_chunks = PAGES_PER_STEP, CHUNKS_PER_STEP
    nc = page_table.shape[1] // g_n
    per_seq = nc // n_chunks
    heads2 = d // LANES
    assert batch * heads2 * steps == nb * per_seq
    i32 = jnp.int32

    idx = jnp.arange(t, dtype=i32)
    tri = (idx[:, None] > idx[None, :]).astype(BF16)
    key = jnp.arange(PAGE, dtype=i32)
    u = (key[:, None] > key[None, :]).astype(BF16)
    ones = jnp.ones((LANES, LANES), BF16)
    feat_head = jnp.arange(d, dtype=i32) // HEAD_DIM
    hsel = (feat_head[:, None] == jnp.arange(LANES, dtype=i32)[None, :]).astype(BF16)
    bias_col = jnp.broadcast_to(jnp.tile(bias, g_n)[:, None], (g_n * HEADS, PAGE))
    bias_row = jnp.pad(bias, (0, LANES - HEADS)).reshape(1, LANES)

    n = g_n * n_chunks
    n_steps = batch * heads2 * steps
    s_idx = jnp.arange(n_steps, dtype=i32)
    slot = jnp.arange(n, dtype=i32)
    chunk = (s_idx % per_seq)[:, None] * n_chunks + (slot // g_n)[None, :]
    step_pages = page_table[(s_idx // per_seq)[:, None], (nc - 1 - chunk) * g_n + (slot % g_n)[None, :]]
    step_pages = step_pages.reshape(n_steps * n).astype(i32)
    log2_per_seq = per_seq.bit_length() - 1
    assert per_seq == 1 << log2_per_seq

    def step_of(b, h, i):
        return (b * heads2 + h) * steps + i

    const2 = lambda b, h, i, sp, _: (0, 0)
    row3 = pl.BlockSpec(
        (None, 1, d), lambda b, h, i, sp, _: (lax.shift_right_logical(step_of(b, h, i), log2_per_seq), 0, 0))
    pool = pl.BlockSpec(memory_space=pl.ANY)
    grid_spec = pltpu.PrefetchScalarGridSpec(
        num_scalar_prefetch=2,
        grid=(batch, heads2, steps),
        in_specs=[pl.BlockSpec((2 * t, LANES), lambda b, h, i, pt, _: (b * steps + i, h)),
                  pl.BlockSpec((None, nq, LANES, t), lambda b, h, i, pt, _: (b, 0, h, 0)),
                  pl.BlockSpec((None, nq, LANES, t), lambda b, h, i, pt, _: (b, 0, h, 0)),
                  pl.BlockSpec((t, t), const2),
                  row3, row3, row3,
                  pl.BlockSpec((g_n * HEADS, PAGE), const2),
                  pl.BlockSpec((1, LANES), const2),
                  pl.BlockSpec((PAGE, PAGE), const2),
                  pl.BlockSpec((LANES, LANES), const2),
                  pl.BlockSpec((d, LANES), const2),
                  pool, pool],
        out_specs=(pl.BlockSpec((2 * t, LANES), lambda b, h, i, pt, _: (b * steps + i, h)), row3),
        scratch_shapes=[pltpu.VMEM((HEADS, d), BF16),
                        pltpu.VMEM((d, PAGE), F32),
                        pltpu.VMEM((HEADS, PAGE), F32),
                        pltpu.VMEM((g_n * HEADS, PAGE), F32),
                        pltpu.VMEM((CHUNK_RING, g_n, d, PAGE), F32),
                        pltpu.VMEM((CHUNK_RING, g_n, d, PAGE), F32),
                        pltpu.SemaphoreType.DMA((CHUNK_RING,)),
                        pltpu.SemaphoreType.DMA((CHUNK_RING,))],
    )
    o_p, o_s = pl.pallas_call(
        functools.partial(_sb_attention_kernel, per_seq=per_seq),
        out_shape=(jax.ShapeDtypeStruct((batch * seq, d), F32), jax.ShapeDtypeStruct((nb, 1, d), F32)),
        grid_spec=grid_spec,
        compiler_params=_params(("arbitrary", "arbitrary", "arbitrary")),
        name="sb_attention",
    )(step_pages, bias, q, kbt, vb, tri,
      q_s.reshape(nb, 1, d), k_new.reshape(nb, 1, d), v_new.reshape(nb, 1, d),
      bias_col, bias_row, u, ones, hsel, cache_kt, cache_vt)
    return o_p, o_s.reshape(nb, d)


def _att_out_rg_in(o, gate, x, mgate, shift, scale, g, wout_ref, win_ref):
    d = D_MODEL
    y = _dot((o * _silu(gate)).astype(BF16), wout_ref[...])
    x1 = x + mgate * y
    h = (_rmsnorm(x1, g) * (1.0 + scale) + shift).astype(BF16)
    return x1, _dot(h, win_ref[:, 0:d]), _dot(h, win_ref[:, d:2 * d])


def _rg_gates(xc, wa_ref, ba_ref, wx_ref, bx_ref, lam_ref):
    xcb = xc.astype(BF16)
    w = RG_BLOCK_W
    ra = jnp.concatenate([_dot(xcb[:, n * w:(n + 1) * w], wa_ref[n]) for n in range(RG_BLOCKS)], axis=-1)
    rx = jnp.concatenate([_dot(xcb[:, n * w:(n + 1) * w], wx_ref[n]) for n in range(RG_BLOCKS)], axis=-1)
    r = _sigmoid(ra + ba_ref[...])
    ig = _sigmoid(rx + bx_ref[...])
    lam = lam_ref[...]
    log_sig_lam = -(jnp.maximum(-lam, 0.0) + jnp.log1p(jnp.exp(-jnp.abs(lam))))
    log_a = RG_C * r * log_sig_lam
    a = jnp.exp(log_a)
    gap = -jnp.tanh(log_a) * (a * a + 1.0)
    u = (gap * lax.rsqrt(jnp.maximum(gap, F32_TINY))) * (ig * xc)
    return a, u


def _rg_finish(y, gbr, x1, mgate, wout_ref, fg_ref):
    out = _dot((y * _silu(gbr)).astype(BF16), wout_ref[...])
    return _rmsnorm(x1 + mgate * out, fg_ref[...])


def _rg_prompt_kernel(o_ref, gate_ref, xin_ref, mgate0_ref, shift_ref, scale_ref, g_ref, wao_ref, win_ref,
                      mgate_ref, cw_ref, cb_ref, wa_ref, ba_ref, wx_ref, bx_ref, lam_ref, wout_ref, fg_ref,
                      y_ref, tail_ref, hlast_ref, hist_s, a_s, u_s, h_s):
    tt = RG_TILE
    d = D_MODEL
    ti = pl.program_id(1)
    pad = SUBLANES

    @pl.when(ti == 0)
    def _():
        hist_s[...] = jnp.zeros((pad, d), F32)
        h_s[...] = jnp.zeros((SUBLANES, d), F32)

    x1, x, gbr = _att_out_rg_in(o_ref[...], gate_ref[...], xin_ref[...], mgate0_ref[...], shift_ref[...],
                                scale_ref[...], g_ref[...], wao_ref, win_ref)
    full = jnp.concatenate([hist_s[...], x], axis=0)
    xc = cb_ref[...] + cw_ref[3:4, :] * x
    for k in range(1, CONV_W):
        xc = xc + cw_ref[CONV_W - 1 - k:CONV_W - k, :] * pltpu.roll(full, k, axis=0)[pad:pad + tt, :]
    hist_s[...] = x[tt - pad:tt, :]

    a, u = _rg_gates(xc, wa_ref, ba_ref, wx_ref, bx_ref, lam_ref)
    a_s[...] = a
    u_s[...] = u
    sub = lax.broadcasted_iota(jnp.int32, (SUBLANES, d), 0)

    def group(i, h):
        r0 = pl.multiple_of(i * SUBLANES, SUBLANES)
        ag = a_s[pl.ds(r0, SUBLANES), :]
        ug = u_s[pl.ds(r0, SUBLANES), :]
        for s in (1, 2, 4):
            a_prev = jnp.where(sub >= s, pltpu.roll(ag, s, axis=0), 1.0)
            u_prev = jnp.where(sub >= s, pltpu.roll(ug, s, axis=0), 0.0)
            ug = ag * u_prev + ug
            ag = ag * a_prev
        hg = ag * h + ug
        u_s[pl.ds(r0, SUBLANES), :] = hg
        return jnp.broadcast_to(hg[SUBLANES - 1:SUBLANES, :], (SUBLANES, d))

    h = lax.fori_loop(0, tt // SUBLANES, group, h_s[...], unroll=4)
    h_s[...] = h

    y_ref[...] = _rg_finish(u_s[...], gbr, x1, mgate_ref[...], wout_ref, fg_ref)

    @pl.when(ti == pl.num_programs(1) - 1)
    def _():
        tail_ref[...] = x[tt - (CONV_W - 1):tt, :]
        hlast_ref[...] = h[0:1, :]


def _rg_prompt(o, gate, x, mgate0, shift, scale, g, w_att_out, w_rg_in,
               mgate, cw, cb, wa, ba, wx, bx, lam, wout, fg, *, batch, seq):
    d = D_MODEL
    tt = RG_TILE
    nt = seq // tt
    row = pl.BlockSpec((tt, d), lambda b, t: (b * nt + t, 0))
    per_seq = pl.BlockSpec((None, 1, d), lambda b, t: (b, 0, 0))
    vec = _resident((1, d))
    gates_w = _resident((RG_BLOCKS, RG_BLOCK_W, RG_BLOCK_W))
    return pl.pallas_call(
        _rg_prompt_kernel,
        out_shape=(jax.ShapeDtypeStruct((batch * seq, d), F32),
                   jax.ShapeDtypeStruct((batch, CONV_W - 1, d), F32),
                   jax.ShapeDtypeStruct((batch, 1, d), F32)),
        grid=(batch, nt),
        in_specs=[row, row, row, per_seq, per_seq, per_seq, vec, _resident((d, d)), _resident((d, 2 * d)),
                  per_seq,
                  _resident((CONV_W, d)),
                  vec, gates_w, vec, gates_w, vec, vec,
                  _resident((d, d)),
                  vec],
        out_specs=(row,
                   pl.BlockSpec((None, CONV_W - 1, d), lambda b, t: (b, 0, 0)),
                   per_seq),
        scratch_shapes=[pltpu.VMEM((SUBLANES, d), F32),
                        pltpu.VMEM((tt, d), F32),
                        pltpu.VMEM((tt, d), F32),
                        pltpu.VMEM((SUBLANES, d), F32)],
        compiler_params=_params(("arbitrary", "arbitrary")),
        name="rg_prompt",
    )(o, gate, x, mgate0, shift, scale, g, w_att_out, w_rg_in, mgate, cw, cb, wa, ba, wx, bx, lam, wout, fg)


def _rg_sample_kernel(o_ref, gate_ref, xin_ref, mgate0_ref, shift_ref, scale_ref, g_ref, wao_ref, win_ref,
                      mgate_ref, sconv_ref, sh_ref, cw_ref, cb_ref,
                      wa_ref, ba_ref, wx_ref, bx_ref, lam_ref, wout_ref, fg_ref,
                      y_ref, tail_ref, h_ref):
    x1, x, gbr = _att_out_rg_in(o_ref[...], gate_ref[...], xin_ref[...], mgate0_ref[...], shift_ref[...],
                                scale_ref[...], g_ref[...], wao_ref, win_ref)
    xc = cb_ref[...] + cw_ref[0:1, :] * sconv_ref[0]
    xc = xc + cw_ref[1:2, :] * sconv_ref[1]
    xc = xc + cw_ref[2:3, :] * sconv_ref[2]
    xc = xc + cw_ref[3:4, :] * x
    a, u = _rg_gates(xc, wa_ref, ba_ref, wx_ref, bx_ref, lam_ref)
    h = a * sh_ref[...] + u
    h_ref[...] = h
    tail_ref[0] = sconv_ref[1]
    tail_ref[1] = sconv_ref[2]
    tail_ref[2] = x
    y_ref[...] = _rg_finish(h, gbr, x1, mgate_ref[...], wout_ref, fg_ref)


def _rg_sample(o, gate, x, mgate0, shift, scale, g, w_att_out, w_rg_in,
               mgate, sconv, sh, cw, cb, wa, ba, wx, bx, lam, wout, fg):
    nb, d = x.shape
    sds = jax.ShapeDtypeStruct((nb, d), F32)
    return pl.pallas_call(
        _rg_sample_kernel,
        out_shape=(sds, jax.ShapeDtypeStruct((CONV_W - 1, nb, d), F32), sds),
        compiler_params=pltpu.CompilerParams(vmem_limit_bytes=VMEM_LIMIT),
        name="rg_sample",
    )(o, gate, x, mgate0, shift, scale, g, w_att_out, w_rg_in, mgate, sconv, sh,
      cw, cb, wa, ba, wx, bx, lam, wout, fg)


def kernel(x_prompt, x_sample, c_prompt, c_sample, cache_k, cache_v, page_table, state_conv, state_h,
           norm_g, mod_w, mod_b, att_w_in, att_bias, att_w_out, rg_w_in, rg_conv_w, rg_conv_b,
           rg_w_a, rg_b_a, rg_w_x, rg_b_x, rg_lambda, rg_w_out, final_g):
    batch, seq, d = x_prompt.shape
    nb, dec_seq, _ = x_sample.shape
    assert d == D_MODEL and dec_seq == 1 and mod_w.shape[0] == 2

    mod = _modulation(jnp.concatenate([c_prompt, c_sample], axis=0), mod_w, mod_b)

    def mod_parts(layer, lo, hi, shape):
        m = mod[layer, lo:hi]
        return tuple(m[:, i * d:(i + 1) * d].reshape(shape) for i in range(3))

    shift0p, scale0p, gate0p = mod_parts(0, 0, batch, (batch, 1, d))
    shift1p, scale1p, gate1p = mod_parts(1, 0, batch, (batch, 1, d))
    shift0s, scale0s, gate0s = mod_parts(0, batch, batch + nb, (nb, d))
    shift1s, scale1s, gate1s = mod_parts(1, batch, batch + nb, (nb, d))

    g0, g1 = norm_g[0].reshape(1, d), norm_g[1].reshape(1, d)
    w_att_in = att_w_in[0].astype(BF16)
    w_kv_t = jnp.swapaxes(att_w_in[0][:, d:3 * d], 0, 1).astype(BF16)
    w_att_out = att_w_out[0].astype(BF16)
    w_rg_in = rg_w_in[0].astype(BF16)
    w_rg_out = rg_w_out[0].astype(BF16)
    wa, wx = rg_w_a[0].astype(BF16), rg_w_x[0].astype(BF16)
    vec = lambda v: v.reshape(1, d)
    rg_consts = (rg_conv_w[0], vec(rg_conv_b[0]), wa, vec(rg_b_a[0]), wx, vec(rg_b_x[0]),
                 vec(rg_lambda[0]), w_rg_out, vec(final_g))
    bias = att_bias[0]

    xp = x_prompt.reshape(batch * seq, d)
    xs = x_sample.reshape(nb, d)
    q_p, kt_p, vt_p, gate_p, kbt_p, vb_p = _att_in_prompt(xp, shift0p, scale0p, g0, w_att_in, w_kv_t,
                                                          batch=batch, seq=seq)
    q_s, k_s, v_s, gate_s = _att_in_sample(xs, shift0s, scale0s, g0, w_att_in)
    pages_t = lambda cache: jnp.transpose(cache, (0, 1, 3, 4, 2)).reshape(-1, d, PAGE)
    o_p, o_s = _sb_attention(page_table, bias, q_p, kbt_p, vb_p, q_s, k_s, v_s,
                             pages_t(cache_k), pages_t(cache_v), batch=batch, seq=seq)

    y_p, tail_p, hlast_p = _rg_prompt(o_p, gate_p, xp, gate0p, shift1p, scale1p, g1, w_att_out, w_rg_in,
                                      gate1p, *rg_consts, batch=batch, seq=seq)
    y_s, tail_s, h_s = _rg_sample(o_s, gate_s, xs, gate0s, shift1s, scale1s, g1, w_att_out, w_rg_in,
                                  gate1s, jnp.swapaxes(state_conv[0], 0, 1), state_h[0], *rg_consts)

    def kv_rows(t):
        return jnp.transpose(t.reshape(1, batch, HEADS, HEAD_DIM, seq), (0, 1, 4, 2, 3))

    kv_s = (1, nb, 1, HEADS, HEAD_DIM)
    return (y_p.reshape(batch, seq, d), y_s.reshape(nb, 1, d),
            kv_rows(kt_p), kv_rows(vt_p), k_s.reshape(kv_s), v_s.reshape(kv_s),
            tail_p.reshape(1, batch, CONV_W - 1, d), hlast_p.reshape(1, batch, d),
            jnp.swapaxes(tail_s, 0, 1).reshape(1, nb, CONV_W - 1, d), h_s.reshape(1, nb, d))
```
